```python
import math
import jax, jax.numpy as jnp
from jax import lax
import numpy as np

D_MODEL = 4096
BATCH = 2
SEQ = 4096
DEPTH = 1
DEC_BATCH = 32
DEC_SEQ = 4
PAST_LEN = 8192
PAGE_SIZE = 128

D_MIX = D_MODEL
D_ATTN = D_MIX // 2
D_CONV = D_MIX - D_ATTN
HEAD_DIM = 64
N_HEADS = D_ATTN // (2 * HEAD_DIM)
CONV_WIDTH = 31
ROPE_THETA = 10000.0
Q_BLOCK = 128
D_IN = 3 * D_ATTN + 2 * D_CONV
PEER_HEADS = 8
N_KEYS = 128
N_EXPERTS = N_KEYS * N_KEYS
PEER_TOPK = 16
D_QUERY = 256
PEER_BLOCK = 128
EPS = 1e-6
NEG_INF = -1e30

kernel_name = "hymba_diffattn_conformer_peer_adaln_step"


def lambda_init(layer_idx):
    return 0.8 - 0.6 * math.exp(-0.3 * layer_idx)


def rmsnorm(x, g):
    xf = x.astype(jnp.float32)
    y = xf * lax.rsqrt(jnp.mean(xf * xf, axis=-1, keepdims=True) + EPS)
    return (y * g.astype(jnp.float32)).astype(x.dtype)


def layernorm(x, g, b):
    xf = x.astype(jnp.float32)
    mu = jnp.mean(xf, axis=-1, keepdims=True)
    xc = xf - mu
    y = xc * lax.rsqrt(jnp.mean(xc * xc, axis=-1, keepdims=True) + EPS)
    return (y * g.astype(jnp.float32) + b.astype(jnp.float32)).astype(x.dtype)


def rope(x, pos):
    inv = ROPE_THETA ** (-jnp.arange(0, HEAD_DIM, 2, dtype=jnp.float32) / HEAD_DIM)
    ang = pos.astype(jnp.float32)[:, None] * inv[None, :]
    cos = jnp.cos(ang)[None, :, None, None, :]
    sin = jnp.sin(ang)[None, :, None, None, :]
    x1, x2 = x[..., : HEAD_DIM // 2], x[..., HEAD_DIM // 2:]
    out = jnp.concatenate([x1 * cos - x2 * sin, x2 * cos + x1 * sin], axis=-1)
    return out.astype(x.dtype)


def project_in(h, pos, w_in):
    b, t, _ = h.shape
    z = jnp.dot(h, w_in)
    q, k, v, a, gate = jnp.split(z, [D_ATTN, 2 * D_ATTN, 3 * D_ATTN, 3 * D_ATTN + D_CONV], axis=-1)
    q = rope(q.reshape(b, t, N_HEADS, 2, HEAD_DIM), pos)
    k = rope(k.reshape(b, t, N_HEADS, 2, HEAD_DIM), pos)
    v = v.reshape(b, t, N_HEADS, 2 * HEAD_DIM)
    u = a * jax.nn.sigmoid(gate)
    return q, k, v, u


def diff_scores(q, k):
    return jnp.einsum("bqhcd,bkhcd->bhcqk", q, k).astype(jnp.float32) * (HEAD_DIM ** -0.5)


def diff_combine(s, lam):
    p = jax.nn.softmax(s, axis=-1)
    return p[:, :, 0] - lam * p[:, :, 1]


def attend_values(a, v):
    return jnp.einsum("bhqk,bkhe->bqhe", a.astype(v.dtype), v)


def prompt_attend(q, k, v, lam):
    b, t = q.shape[:2]
    nb = t // Q_BLOCK
    k_pos = jnp.arange(t, dtype=jnp.int32)

    def block(args):
        q_blk, start = args
        q_pos = start + jnp.arange(Q_BLOCK, dtype=jnp.int32)
        s = jnp.where(k_pos[None, :] <= q_pos[:, None], diff_scores(q_blk, k), NEG_INF)
        return attend_values(diff_combine(s, lam), v)

    qb = jnp.moveaxis(q.reshape(b, nb, Q_BLOCK, N_HEADS, 2, HEAD_DIM), 1, 0)
    o = lax.map(block, (qb, jnp.arange(nb, dtype=jnp.int32) * Q_BLOCK))
    return jnp.moveaxis(o, 0, 1).reshape(b, t, N_HEADS, 2 * HEAD_DIM)


def make_sample_attend(past_k, past_v):
    def attend(q, k, v, lam):
        t = q.shape[1]
        past = past_k.shape[1]
        causal = jnp.tril(jnp.ones((t, t), dtype=bool))
        s = jnp.concatenate([diff_scores(q, past_k), jnp.where(causal, diff_scores(q, k), NEG_INF)], axis=-1)
        a = diff_combine(s, lam)
        return attend_values(a[..., :past], past_v) + attend_values(a[..., past:], v)
    return attend


def conv_branch(u_ext, conv_w, conv_b, ln_g, ln_b):
    y = lax.conv_general_dilated(u_ext, conv_w[:, None, :].astype(u_ext.dtype), (1,), "VALID",
                                 dimension_numbers=("NWC", "WIO", "NWC"), feature_group_count=D_CONV)
    return jax.nn.silu(layernorm(y + conv_b, ln_g, ln_b))


def peer(h, wq, keys, u_tab, v_tab):
    b, t, d = h.shape
    n = b * t
    nb = -(-n // PEER_BLOCK)
    xt = jnp.pad(h.reshape(n, d), ((0, nb * PEER_BLOCK - n), (0, 0)))

    def block(xb):
        q = jnp.dot(xb, wq).reshape(PEER_BLOCK, PEER_HEADS, 2, D_QUERY // 2)
        s = jnp.einsum("nhcd,hckd->nhck", q, keys).astype(jnp.float32)
        sv, si = lax.top_k(s, PEER_TOPK)
        cand = (sv[:, :, 0, :, None] + sv[:, :, 1, None, :]).reshape(PEER_BLOCK, PEER_HEADS, PEER_TOPK * PEER_TOPK)
        cidx = (si[:, :, 0, :, None] * N_KEYS + si[:, :, 1, None, :]).reshape(PEER_BLOCK, PEER_HEADS, PEER_TOPK * PEER_TOPK)
        fv, fp = lax.top_k(cand, PEER_TOPK)
        e = jnp.take_along_axis(cidx, fp, axis=-1).reshape(PEER_BLOCK, PEER_HEADS * PEER_TOPK)
        g = jax.nn.softmax(fv, axis=-1).reshape(PEER_BLOCK, PEER_HEADS * PEER_TOPK)
        act = jax.nn.gelu(jnp.einsum("nd,ned->ne", xb, u_tab[e]), approximate=False)
        coef = g.astype(xb.dtype) * act
        return jnp.einsum("ne,ned->nd", coef, v_tab[e])

    y = lax.map(block, xt.reshape(nb, PEER_BLOCK, d)).reshape(nb * PEER_BLOCK, d)[:n]
    return y.reshape(b, t, d)


def decoder_layer(x, c, pos, conv_prefix, attend, lam_init, w):
    b, t, _ = x.shape
    mod = jnp.dot(jax.nn.silu(c), w["w_ada"]) + w["b_ada"]
    sh1, sc1, g1, sh2, sc2, g2 = jnp.split(mod[:, None, :], 6, axis=-1)
    h = rmsnorm(x, w["norm1_g"]) * (1 + sc1) + sh1
    q, k, v, u = project_in(h, pos, w["w_in"])
    f32 = jnp.float32
    lam = (jnp.exp(jnp.sum(w["lambda_q1"].astype(f32) * w["lambda_k1"].astype(f32)))
           - jnp.exp(jnp.sum(w["lambda_q2"].astype(f32) * w["lambda_k2"].astype(f32))) + lam_init)
    o = attend(q, k, v, lam)
    o = rmsnorm(o, w["subln_g"]) * (1.0 - lam_init)
    u_ext = jnp.concatenate([conv_prefix.astype(u.dtype), u], axis=1)
    yc = conv_branch(u_ext, w["conv_w"], w["conv_b"], w["conv_ln_g"], w["conv_ln_b"])
    mix = jnp.dot(jnp.concatenate([o.reshape(b, t, D_ATTN), yc], axis=-1), w["w_out"])
    x = x + g1 * mix
    h2 = rmsnorm(x, w["norm2_g"]) * (1 + sc2) + sh2
    x = x + g2 * peer(h2, w["peer_wq"], w["peer_keys"], w["peer_u"], w["peer_v"])
    return x, k, v, u_ext[:, -(CONV_WIDTH - 1):]


def setup_inputs(seed: int = 0) -> dict:
    key = jax.random.key(seed)
    ks = jax.random.split(key, 32)
    n_pages = PAST_LEN // PAGE_SIZE
    n_used = DEC_BATCH * n_pages
    n_pool = n_used + max(1, n_used // 4)

    def nrm(k, shape, scale):
        return jax.random.normal(k, shape, jnp.float32) * scale

    def gain(k, shape):
        return 1.0 + nrm(k, shape, 0.02)

    page_table = jax.random.permutation(ks[7], n_pool)[:n_used].reshape(DEC_BATCH, n_pages).astype(jnp.int32)
    return {
        "x_prompt": nrm(ks[0], (BATCH, SEQ, D_MODEL), 1.0),
        "x_sample": nrm(ks[1], (DEC_BATCH, DEC_SEQ, D_MODEL), 1.0),
        "c_prompt": nrm(ks[2], (BATCH, D_MODEL), 1.0),
        "c_sample": nrm(ks[3], (DEC_BATCH, D_MODEL), 1.0),
        "cache_k": nrm(ks[4], (DEPTH, n_pool, PAGE_SIZE, N_HEADS, 2, HEAD_DIM), 1.0),
        "cache_v": nrm(ks[5], (DEPTH, n_pool, PAGE_SIZE, N_HEADS, 2 * HEAD_DIM), 1.0),
        "state_conv": nrm(ks[6], (DEPTH, DEC_BATCH, CONV_WIDTH - 1, D_CONV), 1.0),
        "page_table": page_table,
        "w_ada": nrm(ks[8], (DEPTH, D_MODEL, 6 * D_MODEL), 0.5 * D_MODEL ** -0.5),
        "b_ada": nrm(ks[9], (DEPTH, 6 * D_MODEL), 0.01),
        "norm1_g": gain(ks[10], (DEPTH, D_MODEL)),
        "norm2_g": gain(ks[11], (DEPTH, D_MODEL)),
        "w_in": nrm(ks[12], (DEPTH, D_MODEL, D_IN), D_MODEL ** -0.5),
        "lambda_q1": nrm(ks[13], (DEPTH, HEAD_DIM), 0.1),
        "lambda_k1": nrm(ks[14], (DEPTH, HEAD_DIM), 0.1),
        "lambda_q2": nrm(ks[15], (DEPTH, HEAD_DIM), 0.1),
        "lambda_k2": nrm(ks[16], (DEPTH, HEAD_DIM), 0.1),
        "subln_g": gain(ks[17], (DEPTH, 2 * HEAD_DIM)),
        "conv_w": nrm(ks[18], (DEPTH, CONV_WIDTH, D_CONV), CONV_WIDTH ** -0.5),
        "conv_b": nrm(ks[19], (DEPTH, D_CONV), 0.02),
        "conv_ln_g": gain(ks[20], (DEPTH, D_CONV)),
        "conv_ln_b": nrm(ks[21], (DEPTH, D_CONV), 0.02),
        "w_out": nrm(ks[22], (DEPTH, D_MIX, D_MODEL), D_MIX ** -0.5),
        "peer_wq": nrm(ks[23], (DEPTH, D_MODEL, PEER_HEADS * D_QUERY), D_MODEL ** -0.5),
        "peer_keys": nrm(ks[24], (DEPTH, PEER_HEADS, 2, N_KEYS, D_QUERY // 2), (D_QUERY // 2) ** -0.5),
        "peer_u": nrm(ks[25], (DEPTH, N_EXPERTS, D_MODEL), D_MODEL ** -0.5),
        "peer_v": nrm(ks[26], (DEPTH, N_EXPERTS, D_MODEL), 1.0),
        "normf_g": gain(ks[27], (D_MODEL,)),
    }


def reference(x_prompt, x_sample, c_prompt, c_sample, cache_k, cache_v, state_conv, page_table,
              w_ada, b_ada, norm1_g, norm2_g, w_in, lambda_q1, lambda_k1, lambda_q2, lambda_k2,
              subln_g, conv_w, conv_b, conv_ln_g, conv_ln_b, w_out, peer_wq, peer_keys, peer_u,
              peer_v, normf_g):
    b_p, t_p = x_prompt.shape[:2]
    b_s, t_s = x_sample.shape[:2]
    past = page_table.shape[1] * PAGE_SIZE
    pos_p = jnp.arange(t_p, dtype=jnp.int32)
    pos_s = past + jnp.arange(t_s, dtype=jnp.int32)
    xp, xs = x_prompt, x_sample
    k_p_rows, v_p_rows, conv_p, k_s_rows, v_s_rows, conv_s = [], [], [], [], [], []
    for l in range(DEPTH):
        w = dict(w_ada=w_ada[l], b_ada=b_ada[l], norm1_g=norm1_g[l], norm2_g=norm2_g[l], w_in=w_in[l],
                 lambda_q1=lambda_q1[l], lambda_k1=lambda_k1[l], lambda_q2=lambda_q2[l], lambda_k2=lambda_k2[l],
                 subln_g=subln_g[l], conv_w=conv_w[l], conv_b=conv_b[l], conv_ln_g=conv_ln_g[l],
                 conv_ln_b=conv_ln_b[l], w_out=w_out[l], peer_wq=peer_wq[l], peer_keys=peer_keys[l],
                 peer_u=peer_u[l], peer_v=peer_v[l])
        lam0 = lambda_init(l)
        zero_prefix = jnp.zeros((b_p, CONV_WIDTH - 1, D_CONV), xp.dtype)
        xp, kp, vp, cp = decoder_layer(xp, c_prompt, pos_p, zero_prefix, prompt_attend, lam0, w)
        k_p_rows.append(kp)
        v_p_rows.append(vp)
        conv_p.append(cp)
        past_k = cache_k[l, page_table].reshape(b_s, past, N_HEADS, 2, HEAD_DIM)
        past_v = cache_v[l, page_table].reshape(b_s, past, N_HEADS, 2 * HEAD_DIM)
        xs, ks_, vs_, cs_ = decoder_layer(xs, c_sample, pos_s, state_conv[l],
                                          make_sample_attend(past_k, past_v), lam0, w)
        k_s_rows.append(ks_)
        v_s_rows.append(vs_)
        conv_s.append(cs_)
    y_prompt = rmsnorm(xp, normf_g)
    y_sample = rmsnorm(xs, normf_g)
    return (y_prompt, y_sample, jnp.stack(k_p_rows), jnp.stack(v_p_rows), jnp.stack(conv_p),
            jnp.stack(k_s_rows), jnp.stack(v_s_rows), jnp.stack(conv_s))
```

```python
import functools
import math

import jax
import jax.numpy as jnp
from jax import lax
from jax.experimental import pallas as pl
from jax.experimental.pallas import tpu as pltpu

F32 = jnp.float32
BF16 = jnp.bfloat16

D_MODEL = 4096
D_ATTN = 2048
D_CONV = 2048
HEAD_DIM = 64
HEAD_W = 2 * HEAD_DIM
N_HEADS = D_ATTN // HEAD_W
CONV_WIDTH = 31
CONV_HALO = 32
ROPE_THETA = 10000.0
PAGE_SIZE = 128
PEER_HEADS = 8
N_KEYS = 128
N_EXPERTS = N_KEYS * N_KEYS
PEER_TOPK = 16
D_QUERY = 256
EPS = 1e-6
NEG_INF = -1e30
LANES = 128
VMEM_LIMIT = 56 * 1024 * 1024


def _lambda_init(layer_idx):
    return 0.8 - 0.6 * math.exp(-0.3 * layer_idx)


def _cparams(*sem):
    return pltpu.CompilerParams(dimension_semantics=sem, vmem_limit_bytes=VMEM_LIMIT)


def _dot_nt(a, b):
    return lax.dot_general(a, b, (((1,), (1,)), ((), ())), preferred_element_type=F32)


def _dot_tn(a, b):
    return lax.dot_general(a, b, (((0,), (0,)), ((), ())), preferred_element_type=F32)


def _ada_kernel(c_ref, w_ref, b_ref, o_ref):
    c = c_ref[...]
    a = (c * jax.nn.sigmoid(c)).astype(BF16)
    o_ref[...] = jnp.dot(a, w_ref[...].astype(BF16), preferred_element_type=F32) + b_ref[...]


def _ada(c_all, w_ada, b_ada):
    m, d = c_all.shape
    n = w_ada.shape[1]
    tn = 1024
    return pl.pallas_call(
        _ada_kernel,
        grid=(n // tn,),
        in_specs=[pl.BlockSpec((m, d), lambda j: (0, 0)),
                  pl.BlockSpec((d, tn), lambda j: (0, j)),
                  pl.BlockSpec((1, tn), lambda j: (0, j))],
        out_specs=pl.BlockSpec((m, tn), lambda j: (0, j)),
        out_shape=jax.ShapeDtypeStruct((m, n), F32),
        compiler_params=_cparams("arbitrary"),
        name="ada",
    )(c_all, w_ada, b_ada.reshape(1, n))


def _mod_spec(mod3, tm, tn, rows_per_group, with_j):
    per_token = mod3.shape[1] != 1
    if per_token:
        if with_j:
            return pl.BlockSpec((None, tm, tn), lambda i, j: (0, i, j))
        return pl.BlockSpec((None, tm, tn), lambda i: (0, i, 0))
    tiles_per_group = rows_per_group // tm
    if with_j:
        return pl.BlockSpec((None, 1, tn), lambda i, j: (i // tiles_per_group, 0, j))
    return pl.BlockSpec((None, 1, tn), lambda i: (i // tiles_per_group, 0, 0))


def _rms(x):
    return x * lax.rsqrt(jnp.mean(x * x, axis=-1, keepdims=True) + EPS)


def _norm_mod_kernel(x_ref, g_ref, sc_ref, sh_ref, o_ref):
    y = _rms(x_ref[...]) * g_ref[...]
    o_ref[...] = (y * (1.0 + sc_ref[...]) + sh_ref[...]).astype(o_ref.dtype)


def _norm_mod(x, g, sc3, sh3, tm, rows_per_group):
    n, d = x.shape
    return pl.pallas_call(
        _norm_mod_kernel,
        grid=(n // tm,),
        in_specs=[pl.BlockSpec((tm, d), lambda i: (i, 0)),
                  pl.BlockSpec((1, d), lambda i: (0, 0)),
                  _mod_spec(sc3, tm, d, rows_per_group, False),
                  _mod_spec(sh3, tm, d, rows_per_group, False)],
        out_specs=pl.BlockSpec((tm, d), lambda i: (i, 0)),
        out_shape=jax.ShapeDtypeStruct((n, d), BF16),
        compiler_params=_cparams("arbitrary"),
        name="norm_mod",
    )(x, g.reshape(1, d), sc3, sh3)


def _rope_apply(z, cos_ref, sin_ref):
    cos, sin = cos_ref[...], sin_ref[...]
    lane = lax.broadcasted_iota(jnp.int32, cos.shape, 1)
    first = (lane & (HEAD_DIM - 1)) < HEAD_DIM // 2
    half = HEAD_DIM // 2
    outs = []
    for g in range(z.shape[1] // LANES):
        zg = z[:, g * LANES:(g + 1) * LANES]
        partner = jnp.where(first, pltpu.roll(zg, LANES - half, 1), pltpu.roll(zg, half, 1))
        outs.append(zg * cos + partner * sin)
    return jnp.concatenate(outs, axis=1)


def _proj_kernel(*refs, rope, scale, want_f32, want_bf, want_hm):
    h_ref, w_ref = refs[0], refs[1]
    pos = 2
    z = jnp.dot(h_ref[...], w_ref[...], preferred_element_type=F32)
    if rope:
        z = _rope_apply(z, refs[2], refs[3])
        pos = 4
    outs = refs[pos:]
    k = 0
    if want_f32:
        outs[k][...] = z
        k += 1
    zs = z * scale if scale != 1.0 else z
    if want_bf:
        outs[k][...] = zs.astype(BF16)
        k += 1
    if want_hm:
        for hh in range(z.shape[1] // HEAD_W):
            outs[k][hh] = zs[:, hh * HEAD_W:(hh + 1) * HEAD_W].astype(BF16)


def _proj(h, w_bf, col0, ncols, tm, tn, *, rope=None, scale=1.0, want_f32=False, want_bf=False,
          hm_seq=None):
    n, kdim = h.shape
    joff = col0 // tn
    in_specs = [pl.BlockSpec((tm, kdim), lambda i, j: (i, 0)),
                pl.BlockSpec((kdim, tn), lambda i, j: (0, j + joff))]
    args = [h, w_bf]
    if rope is not None:
        cos, sin = rope
        tiles = cos.shape[0] // tm
        in_specs += [pl.BlockSpec((tm, LANES), lambda i, j: (i % tiles, 0))] * 2
        args += [cos, sin]
    out_specs, out_shape = [], []
    if want_f32:
        out_specs.append(pl.BlockSpec((tm, tn), lambda i, j: (i, j)))
        out_shape.append(jax.ShapeDtypeStruct((n, ncols), F32))
    if want_bf:
        out_specs.append(pl.BlockSpec((tm, tn), lambda i, j: (i, j)))
        out_shape.append(jax.ShapeDtypeStruct((n, ncols), BF16))
    if hm_seq is not None:
        tps = hm_seq // tm
        hpt = tn // HEAD_W
        out_specs.append(pl.BlockSpec((None, hpt, tm, HEAD_W), lambda i, j: (i // tps, j, i % tps, 0)))
        out_shape.append(jax.ShapeDtypeStruct((n // hm_seq, ncols // HEAD_W, hm_seq, HEAD_W), BF16))
    kern = functools.partial(_proj_kernel, rope=rope is not None, scale=scale, want_f32=want_f32,
                             want_bf=want_bf, want_hm=hm_seq is not None)
    return pl.pallas_call(
        kern,
        grid=(n // tm, ncols // tn),
        in_specs=in_specs,
        out_specs=out_specs,
        out_shape=out_shape,
        compiler_params=_cparams("arbitrary", "arbitrary"),
        name="proj",
    )(*args)


def _glu_kernel(h_ref, wa_ref, wg_ref, o_ref):
    h = h_ref[...]
    a = jnp.dot(h, wa_ref[...], preferred_element_type=F32)
    g = jnp.dot(h, wg_ref[...], preferred_element_type=F32)
    o_ref[...] = a * jax.nn.sigmoid(g)


def _glu(h, w_bf, col_a, col_g, ncols, tm, tn):
    n, kdim = h.shape
    ja, jg = col_a // tn, col_g // tn
    return pl.pallas_call(
        _glu_kernel,
        grid=(n // tm, ncols // tn),
        in_specs=[pl.BlockSpec((tm, kdim), lambda i, j: (i, 0)),
                  pl.BlockSpec((kdim, tn), lambda i, j: (0, j + ja)),
                  pl.BlockSpec((kdim, tn), lambda i, j: (0, j + jg))],
        out_specs=pl.BlockSpec((tm, tn), lambda i, j: (i, j)),
        out_shape=jax.ShapeDtypeStruct((n, ncols), F32),
        compiler_params=_cparams("arbitrary", "arbitrary"),
        name="glu",
    )(h, w_bf, w_bf)


def _lam_value(lq1, lk1, lq2, lk2, lam_init):
    a = jnp.exp(jnp.sum(lq1[...] * lk1[...], axis=-1, keepdims=True))
    b = jnp.exp(jnp.sum(lq2[...] * lk2[...], axis=-1, keepdims=True))
    return a - b + lam_init


def _online_update(s, v, m, l, acc):
    m_new = jnp.maximum(m, jnp.max(s, axis=-1, keepdims=True))
    alpha = jnp.exp(m - m_new)
    p = jnp.exp(s - m_new)
    l_new = alpha * l + jnp.sum(p, axis=-1, keepdims=True)
    acc_new = alpha * acc + jnp.dot(p.astype(BF16), v, preferred_element_type=F32)
    return m_new, l_new, acc_new


def _prompt_attn_kernel(lq1, lk1, lq2, lk2, g_ref, q_ref, k_ref, v_ref, o_ref, *, tq, lam_init):
    qi = pl.program_id(2)
    lam = _lam_value(lq1, lk1, lq2, lk2, lam_init)
    q = q_ref[...]
    lane = lax.broadcasted_iota(jnp.int32, q.shape, 1)
    zero = jnp.zeros_like(q)
    q1 = jnp.where(lane < HEAD_DIM, q, zero)
    q2 = jnp.where(lane >= HEAD_DIM, q, zero)

    def step(j, carry, masked):
        m1, l1, a1, m2, l2, a2 = carry
        start = pl.multiple_of(j * tq, tq)
        kj = k_ref[pl.ds(start, tq), :]
        vj = v_ref[pl.ds(start, tq), :]
        s1 = _dot_nt(q1, kj)
        s2 = _dot_nt(q2, kj)
        if masked:
            row = lax.broadcasted_iota(jnp.int32, s1.shape, 0)
            col = lax.broadcasted_iota(jnp.int32, s1.shape, 1)
            keep = col <= row
            s1 = jnp.where(keep, s1, NEG_INF)
            s2 = jnp.where(keep, s2, NEG_INF)
        m1, l1, a1 = _online_update(s1, vj, m1, l1, a1)
        m2, l2, a2 = _online_update(s2, vj, m2, l2, a2)
        return m1, l1, a1, m2, l2, a2

    mi = jnp.full((tq, 1), NEG_INF, F32)
    li = jnp.zeros((tq, 1), F32)
    ai = jnp.zeros((tq, HEAD_W), F32)
    carry = lax.fori_loop(0, qi, lambda j, c: step(j, c, False), (mi, li, ai, mi, li, ai))
    m1, l1, a1, m2, l2, a2 = step(qi, carry, True)
    o = a1 / l1 - lam * (a2 / l2)
    o_ref[...] = (_rms(o) * g_ref[...] * (1.0 - lam_init)).astype(o_ref.dtype)


def _prompt_attn(q_hm, k_hm, v_hm, lam_params, subln_g, lam_init, tq):
    b, h, t, _ = q_hm.shape
    vec = pl.BlockSpec((1, HEAD_DIM), lambda bi, hi, qi: (0, 0))
    return pl.pallas_call(
        functools.partial(_prompt_attn_kernel, tq=tq, lam_init=lam_init),
        grid=(b, h, t // tq),
        in_specs=[vec, vec, vec, vec,
                  pl.BlockSpec((1, HEAD_W), lambda bi, hi, qi: (0, 0)),
                  pl.BlockSpec((None, None, tq, HEAD_W), lambda bi, hi, qi: (bi, hi, qi, 0)),
                  pl.BlockSpec((None, None, t, HEAD_W), lambda bi, hi, qi: (bi, hi, 0, 0)),
                  pl.BlockSpec((None, None, t, HEAD_W), lambda bi, hi, qi: (bi, hi, 0, 0))],
        out_specs=pl.BlockSpec((None, tq, HEAD_W), lambda bi, hi, qi: (bi, qi, hi)),
        out_shape=jax.ShapeDtypeStruct((b, t, h * HEAD_W), BF16),
        compiler_params=_cparams("arbitrary", "arbitrary", "arbitrary"),
        name="prompt_attn",
    )(*lam_params, subln_g.reshape(1, HEAD_W), q_hm, k_hm, v_hm)


def _sample_attn_kernel(pt_ref, lq1, lk1, lq2, lk2, g_ref, qrep_ref, knew_ref, vnew_ref, kp_ref, vp_ref,
                        o_ref, qbd_s, m_s, l_s, acc_s, *, t_new, lam_init):
    del pt_ref
    p = pl.program_id(1)
    rows = qrep_ref.shape[0]

    def update(kb, vb, keep):
        s = _dot_nt(qbd_s[...], kb)
        if keep is not None:
            s = jnp.where(keep, s, NEG_INF)
        m, l, acc = _online_update(s, vb, m_s[...], l_s[...], acc_s[...])
        m_s[...] = m
        l_s[...] = l
        acc_s[...] = acc

    @pl.when(p == 0)
    def _():
        row = lax.broadcasted_iota(jnp.int32, qrep_ref.shape, 0)
        lane = lax.broadcasted_iota(jnp.int32, qrep_ref.shape, 1)
        own = (lane // HEAD_DIM) == (row // t_new)
        qbd_s[...] = jnp.where(own, qrep_ref[...], jnp.zeros_like(qrep_ref[...]))
        m_s[...] = jnp.full(m_s.shape, NEG_INF, F32)
        l_s[...] = jnp.zeros(l_s.shape, F32)
        acc_s[...] = jnp.zeros(acc_s.shape, F32)
        r = lax.broadcasted_iota(jnp.int32, (rows, PAGE_SIZE), 0)
        c = lax.broadcasted_iota(jnp.int32, (rows, PAGE_SIZE), 1)
        keep = (c < t_new) & (c <= (r % t_new))
        update(knew_ref[...], vnew_ref[...], keep)

    update(kp_ref[...].astype(BF16), vp_ref[...].astype(BF16), None)

    @pl.when(p == pl.num_programs(1) - 1)
    def _():
        lam = _lam_value(lq1, lk1, lq2, lk2, lam_init)
        g = g_ref[...]
        grp = 2 * t_new
        for h in range(N_HEADS):
            blk = acc_s[h * grp:(h + 1) * grp, h * HEAD_W:(h + 1) * HEAD_W]
            blk = blk / l_s[h * grp:(h + 1) * grp, :]
            o = blk[0:t_new] - lam * blk[t_new:grp]
            o_ref[:, h * HEAD_W:(h + 1) * HEAD_W] = _rms(o) * g * (1.0 - lam_init)


def _sample_attn(page_table, qrep, knew, vnew, cache_k2, cache_v2, lam_params, subln_g, lam_init, t_new):
    bs, n_pages = page_table.shape
    rows = qrep.shape[1]
    d = qrep.shape[2]
    vec = pl.BlockSpec((1, HEAD_DIM), lambda b, p, pt: (0, 0))
    grid_spec = pltpu.PrefetchScalarGridSpec(
        num_scalar_prefetch=1,
        grid=(bs, n_pages),
        in_specs=[vec, vec, vec, vec,
                  pl.BlockSpec((1, HEAD_W), lambda b, p, pt: (0, 0)),
                  pl.BlockSpec((None, rows, d), lambda b, p, pt: (b, 0, 0)),
                  pl.BlockSpec((None, PAGE_SIZE, d), lambda b, p, pt: (b, 0, 0)),
                  pl.BlockSpec((None, PAGE_SIZE, d), lambda b, p, pt: (b, 0, 0)),
                  pl.BlockSpec((None, PAGE_SIZE, d), lambda b, p, pt: (pt[b, p], 0, 0)),
                  pl.BlockSpec((None, PAGE_SIZE, d), lambda b, p, pt: (pt[b, p], 0, 0))],
        out_specs=pl.BlockSpec((None, t_new, d), lambda b, p, pt: (b, 0, 0)),
        scratch_shapes=[pltpu.VMEM((rows, d), BF16),
                        pltpu.VMEM((rows, 1), F32),
                        pltpu.VMEM((rows, 1), F32),
                        pltpu.VMEM((rows, d), F32)],
    )
    return pl.pallas_call(
        functools.partial(_sample_attn_kernel, t_new=t_new, lam_init=lam_init),
        grid_spec=grid_spec,
        out_shape=jax.ShapeDtypeStruct((bs, t_new, d), F32),
        compiler_params=_cparams("arbitrary", "arbitrary"),
        name="sample_attn",
    )(page_table, *lam_params, subln_g.reshape(1, HEAD_W), qrep, knew, vnew, cache_k2, cache_v2)


def _ln_swish(y, lg_ref, lb_ref):
    mu = jnp.mean(y, axis=-1, keepdims=True)
    yc = y - mu
    z = yc * lax.rsqrt(jnp.mean(yc * yc, axis=-1, keepdims=True) + EPS) * lg_ref[...] + lb_ref[...]
    return z * jax.nn.sigmoid(z)


def _conv_prompt_kernel(halo_ref, u_ref, cw_ref, cb_ref, lg_ref, lb_ref, o_ref, ext_s, y_s, *, tt, cc):
    i = pl.program_id(1)
    ext_s[CONV_HALO:, :] = u_ref[...]

    @pl.when(i == 0)
    def _():
        ext_s[0:CONV_HALO, :] = jnp.zeros((CONV_HALO, ext_s.shape[1]), F32)

    @pl.when(i > 0)
    def _():
        ext_s[0:CONV_HALO, :] = halo_ref[...]

    first = CONV_HALO - (CONV_WIDTH - 1)

    def chunk(ci, carry):
        c0 = pl.multiple_of(ci * cc, cc)
        acc = jnp.zeros((tt, cc), F32)
        for w in range(CONV_WIDTH):
            acc = acc + ext_s[pl.ds(first + w, tt), pl.ds(c0, cc)] * cw_ref[w:w + 1, pl.ds(c0, cc)]
        y_s[:, pl.ds(c0, cc)] = acc + cb_ref[:, pl.ds(c0, cc)]
        return carry

    lax.fori_loop(0, ext_s.shape[1] // cc, chunk, 0)
    o_ref[...] = _ln_swish(y_s[...], lg_ref, lb_ref).astype(o_ref.dtype)


def _conv_prompt(u3, conv_w, conv_b, ln_g, ln_b, tt):
    b, t, c = u3.shape
    hb = tt // CONV_HALO
    row = pl.BlockSpec((1, c), lambda bi, i: (0, 0))
    return pl.pallas_call(
        functools.partial(_conv_prompt_kernel, tt=tt, cc=256),
        grid=(b, t // tt),
        in_specs=[pl.BlockSpec((None, CONV_HALO, c), lambda bi, i: (bi, jnp.maximum(i * hb - 1, 0), 0)),
                  pl.BlockSpec((None, tt, c), lambda bi, i: (bi, i, 0)),
                  pl.BlockSpec((CONV_WIDTH, c), lambda bi, i: (0, 0)),
                  row, row, row],
        out_specs=pl.BlockSpec((None, tt, c), lambda bi, i: (bi, i, 0)),
        out_shape=jax.ShapeDtypeStruct((b, t, c), BF16),
        scratch_shapes=[pltpu.VMEM((CONV_HALO + tt, c), F32), pltpu.VMEM((tt, c), F32)],
        compiler_params=_cparams("arbitrary", "arbitrary"),
        name="conv_prompt",
    )(u3, u3, conv_w, conv_b.reshape(1, c), ln_g.reshape(1, c), ln_b.reshape(1, c))


def _conv_sample_kernel(ext_ref, cw_ref, cb_ref, lg_ref, lb_ref, o_ref, *, t_new):
    acc = jnp.zeros((t_new, ext_ref.shape[1]), F32)
    for w in range(CONV_WIDTH):
        acc = acc + ext_ref[w:w + t_new, :] * cw_ref[w:w + 1, :]
    o_ref[...] = _ln_swish(acc + cb_ref[...], lg_ref, lb_ref)


def _conv_sample(ext3, conv_w, conv_b, ln_g, ln_b, t_new):
    b, rows, c = ext3.shape
    row = pl.BlockSpec((1, c), lambda bi: (0, 0))
    return pl.pallas_call(
        functools.partial(_conv_sample_kernel, t_new=t_new),
        grid=(b,),
        in_specs=[pl.BlockSpec((None, rows, c), lambda bi: (bi, 0, 0)),
                  pl.BlockSpec((CONV_WIDTH, c), lambda bi: (0, 0)),
                  row, row, row],
        out_specs=pl.BlockSpec((None, t_new, c), lambda bi: (bi, 0, 0)),
        out_shape=jax.ShapeDtypeStruct((b, t_new, c), F32),
        compiler_params=_cparams("arbitrary"),
        name="conv_sample",
    )(ext3, conv_w, conv_b.reshape(1, c), ln_g.reshape(1, c), ln_b.reshape(1, c))


def _out_kernel(o_ref, yc_ref, wt_ref, wb_ref, x_ref, g_ref, out_ref):
    mix = (jnp.dot(o_ref[...], wt_ref[...], preferred_element_type=F32)
           + jnp.dot(yc_ref[...], wb_ref[...], preferred_element_type=F32))
    out_ref[...] = x_ref[...] + g_ref[...] * mix


def _out_proj(o_bf, yc_bf, w_out_bf, x, g3, tm, tn, rows_per_group):
    n, ka = o_bf.shape
    kc = yc_bf.shape[1]
    d = w_out_bf.shape[1]
    assert ka == kc
    return pl.pallas_call(
        _out_kernel,
        grid=(n // tm, d // tn),
        in_specs=[pl.BlockSpec((tm, ka), lambda i, j: (i, 0)),
                  pl.BlockSpec((tm, kc), lambda i, j: (i, 0)),
                  pl.BlockSpec((ka, tn), lambda i, j: (0, j)),
                  pl.BlockSpec((kc, tn), lambda i, j: (1, j)),
                  pl.BlockSpec((tm, tn), lambda i, j: (i, j)),
                  _mod_spec(g3, tm, tn, rows_per_group, True)],
        out_specs=pl.BlockSpec((tm, tn), lambda i, j: (i, j)),
        out_shape=jax.ShapeDtypeStruct((n, d), F32),
        compiler_params=_cparams("arbitrary", "arbitrary"),
        name="out_proj",
    )(o_bf, yc_bf, w_out_bf, w_out_bf, x, g3)


def _drop_first_max(work, rid, fill):
    m = jnp.max(work, axis=0, keepdims=True)
    first = jnp.min(jnp.where(work == m, rid, float(work.shape[0])), axis=0, keepdims=True)
    return m, jnp.where(rid == first, fill, work)


def _row_ids(shape):
    return lax.broadcasted_iota(jnp.int32, shape, 0).astype(F32)


def _top_rows(s, k):
    rid = _row_ids(s.shape)
    rows = []
    work = s
    for _ in range(k):
        m, work = _drop_first_max(work, rid, -jnp.inf)
        rows.append(m)
    return jnp.concatenate(rows, axis=0)


def _kth_largest(c, k):
    rid = _row_ids(c.shape)
    work = c
    for _ in range(k - 1):
        _, work = _drop_first_max(work, rid, -1.0)
    return jnp.max(work, axis=0, keepdims=True)


def _candidates(pa, pb):
    parts = [pa[0:1] * pb]
    for a in range(1, 8):
        parts.append(pa[a:a + 1] * pb[0:8])
    parts.append(pa[8:16] * pb[0:1])
    return jnp.concatenate(parts, axis=0)


def _peer_route_kernel(q_ref, keys_ref, ea_ref, eb_ref, pt_ref):
    q = q_ref[...].astype(BF16)
    half = D_QUERY // 2
    pts = []
    for h in range(PEER_HEADS):
        s1 = _dot_nt(keys_ref[h, 0], q[:, (2 * h) * half:(2 * h + 1) * half])
        s2 = _dot_nt(keys_ref[h, 1], q[:, (2 * h + 1) * half:(2 * h + 2) * half])
        sv1 = _top_rows(s1, PEER_TOPK)
        sv2 = _top_rows(s2, PEER_TOPK)
        m1, m2 = sv1[0:1], sv2[0:1]
        ea = jnp.exp(s1 - m1)
        eb = jnp.exp(s2 - m2)
        pa = jnp.exp(sv1 - m1)
        pb = jnp.exp(sv2 - m2)
        cand = _candidates(pa, pb)
        sel = cand >= _kth_largest(cand, PEER_TOPK)
        zinv = 1.0 / jnp.sum(jnp.where(sel, cand, 0.0), axis=0, keepdims=True)
        cand_n = _candidates(pa, pb * zinv)
        pts.append(jnp.min(jnp.where(sel, cand_n, jnp.inf), axis=0, keepdims=True))
        ea_ref[h] = ea
        eb_ref[h] = eb * zinv
    pt_ref[...] = jnp.concatenate(pts, axis=0)


def _peer_route(q, keys_bf, tt):
    n, dq = q.shape
    return pl.pallas_call(
        _peer_route_kernel,
        grid=(n // tt,),
        in_specs=[pl.BlockSpec((tt, dq), lambda i: (i, 0)),
                  pl.BlockSpec(keys_bf.shape, lambda i: (0, 0, 0, 0))],
        out_specs=[pl.BlockSpec((PEER_HEADS, N_KEYS, tt), lambda i: (0, 0, i)),
                   pl.BlockSpec((PEER_HEADS, N_KEYS, tt), lambda i: (0, 0, i)),
                   pl.BlockSpec((PEER_HEADS, tt), lambda i: (0, i))],
        out_shape=[jax.ShapeDtypeStruct((PEER_HEADS, N_KEYS, n), F32),
                   jax.ShapeDtypeStruct((PEER_HEADS, N_KEYS, n), F32),
                   jax.ShapeDtypeStruct((PEER_HEADS, n), F32)],
        compiler_params=_cparams("arbitrary"),
        name="peer_route",
    )(q, keys_bf)


def _gelu_exact(x):
    return 0.5 * x * (1.0 + lax.erf(x * (2.0 ** -0.5)))


def _peer_dense_kernel(h_ref, u_ref, v_ref, ea_ref, eb_ref, pt_ref, y_ref, *, ec):
    c = pl.program_id(1)

    @pl.when(c == 0)
    def _():
        y_ref[...] = jnp.zeros(y_ref.shape, F32)

    act = _gelu_exact(_dot_nt(u_ref[...], h_ref[...]))
    groups = ec // N_KEYS
    pieces = []
    for ii in range(groups):
        i = c * groups + ii
        gate = jnp.zeros((N_KEYS, act.shape[1]), F32)
        for h in range(PEER_HEADS):
            p = ea_ref[h, pl.ds(i, 1), :] * eb_ref[h]
            gate = gate + jnp.where(p >= pt_ref[h:h + 1, :], p, 0.0)
        pieces.append((gate * act[ii * N_KEYS:(ii + 1) * N_KEYS]).astype(BF16))
    coef = jnp.concatenate(pieces, axis=0)
    y_ref[...] += _dot_tn(coef, v_ref[...])


def _peer_dense(h2, u_bf, v_bf, ea, eb, pt, tt, ec):
    n, d = h2.shape
    ne = u_bf.shape[0]
    return pl.pallas_call(
        functools.partial(_peer_dense_kernel, ec=ec),
        grid=(n // tt, ne // ec),
        in_specs=[pl.BlockSpec((tt, d), lambda i, c: (i, 0)),
                  pl.BlockSpec((ec, d), lambda i, c: (c, 0)),
                  pl.BlockSpec((ec, d), lambda i, c: (c, 0)),
                  pl.BlockSpec((PEER_HEADS, N_KEYS, tt), lambda i, c: (0, 0, i)),
                  pl.BlockSpec((PEER_HEADS, N_KEYS, tt), lambda i, c: (0, 0, i)),
                  pl.BlockSpec((PEER_HEADS, tt), lambda i, c: (0, i))],
        out_specs=pl.BlockSpec((tt, d), lambda i, c: (i, 0)),
        out_shape=jax.ShapeDtypeStruct((n, d), F32),
        compiler_params=_cparams("arbitrary", "arbitrary"),
        name="peer_dense",
    )(h2, u_bf, v_bf, ea, eb, pt)


def _final_kernel(x_ref, y_ref, g2_ref, gf_ref, o_ref):
    x = x_ref[...] + g2_ref[...] * y_ref[...]
    o_ref[...] = _rms(x) * gf_ref[...]


def _final(x1, y, g23, normf_g, tm, rows_per_group):
    n, d = x1.shape
    return pl.pallas_call(
        _final_kernel,
        grid=(n // tm,),
        in_specs=[pl.BlockSpec((tm, d), lambda i: (i, 0)),
                  pl.BlockSpec((tm, d), lambda i: (i, 0)),
                  _mod_spec(g23, tm, d, rows_per_group, False),
                  pl.BlockSpec((1, d), lambda i: (0, 0))],
        out_specs=pl.BlockSpec((tm, d), lambda i: (i, 0)),
        out_shape=jax.ShapeDtypeStruct((n, d), F32),
        compiler_params=_cparams("arbitrary"),
        name="final",
    )(x1, y, g23, normf_g.reshape(1, d))


def _rope_tables(pos):
    inv = ROPE_THETA ** (-jnp.arange(0, HEAD_DIM, 2, dtype=F32) / HEAD_DIM)
    ang = pos.astype(F32)[:, None] * inv[None, :]
    cos, sin = jnp.cos(ang), jnp.sin(ang)
    cos_t = jnp.concatenate([cos, cos, cos, cos], axis=1)
    sin_t = jnp.concatenate([-sin, sin, -sin, sin], axis=1)
    return cos_t, sin_t


def _peer_block(h2, x1, g23, w, tm_rows, rows_per_group, tt_route, tt_dense, ec):
    q = _proj(h2, w["wq"], 0, w["wq"].shape[1], tm_rows, 512, want_f32=True)[0]
    ea, eb, pt = _peer_route(q, w["keys"], tt_route)
    y = _peer_dense(h2, w["u"], w["v"], ea, eb, pt, tt_dense, ec)
    return _final(x1, y, g23, w["normf_g"], min(tm_rows, 256), rows_per_group)


def kernel(x_prompt, x_sample, c_prompt, c_sample, cache_k, cache_v, state_conv, page_table, w_ada, b_ada,
           norm1_g, norm2_g, w_in, lambda_q1, lambda_k1, lambda_q2, lambda_k2, subln_g, conv_w, conv_b,
           conv_ln_g, conv_ln_b, w_out, peer_wq, peer_keys, peer_u, peer_v, normf_g):
    depth = w_ada.shape[0]
    assert depth == 1, "single-layer trunk"
    layer = 0
    lam_init = _lambda_init(layer)
    bp, tp, d = x_prompt.shape
    bs, ts, _ = x_sample.shape
    n_p, n_s = bp * tp, bs * ts
    past = page_table.shape[1] * PAGE_SIZE

    w_in_bf = w_in[layer].astype(BF16)
    w_out_bf = w_out[layer].astype(BF16)
    pw = dict(wq=peer_wq[layer].astype(BF16), keys=peer_keys[layer].astype(BF16),
              u=peer_u[layer].astype(BF16), v=peer_v[layer].astype(BF16), normf_g=normf_g)
    lam_params = [p[layer].reshape(1, HEAD_DIM) for p in (lambda_q1, lambda_k1, lambda_q2, lambda_k2)]

    mod = _ada(jnp.concatenate([c_prompt, c_sample], axis=0), w_ada[layer], b_ada[layer])
    mod_p = [m.reshape(bp, 1, d) for m in jnp.split(mod[:bp], 6, axis=-1)]
    mod_s = [jnp.repeat(m, ts, axis=0).reshape(1, n_s, d) for m in jnp.split(mod[bp:], 6, axis=-1)]

    cos_p, sin_p = _rope_tables(jnp.arange(tp, dtype=jnp.int32))
    pos_s = past + jnp.arange(ts, dtype=jnp.int32)
    cos_s, sin_s = _rope_tables(jnp.tile(pos_s, bs))
    qk_scale = HEAD_DIM ** -0.5

    xp = x_prompt.reshape(n_p, d)
    sh1, sc1, g1, sh2, sc2, g2 = mod_p
    tm = 1024
    h = _norm_mod(xp, norm1_g[layer], sc1, sh1, 256, tp)
    (q_hm,) = _proj(h, w_in_bf, 0, D_ATTN, tm, 512, rope=(cos_p, sin_p), scale=qk_scale, hm_seq=tp)
    k_p, k_hm = _proj(h, w_in_bf, D_ATTN, D_ATTN, tm, 512, rope=(cos_p, sin_p), want_f32=True, hm_seq=tp)
    v_p, v_hm = _proj(h, w_in_bf, 2 * D_ATTN, D_ATTN, tm, 512, want_f32=True, hm_seq=tp)
    u_p = _glu(h, w_in_bf, 3 * D_ATTN, 3 * D_ATTN + D_CONV, D_CONV, tm, 512)
    o_p = _prompt_attn(q_hm, k_hm, v_hm, lam_params, subln_g[layer], lam_init, 512)
    u_p3 = u_p.reshape(bp, tp, D_CONV)
    yc_p = _conv_prompt(u_p3, conv_w[layer], conv_b[layer], conv_ln_g[layer], conv_ln_b[layer], 128)
    x1_p = _out_proj(o_p.reshape(n_p, D_ATTN), yc_p.reshape(n_p, D_CONV), w_out_bf, xp, g1, tm, 512, tp)
    h2_p = _norm_mod(x1_p, norm2_g[layer], sc2, sh2, 256, tp)
    y_prompt = _peer_block(h2_p, x1_p, g2, pw, tm, tp, 256, 512, 512).reshape(bp, tp, d)

    xs = x_sample.reshape(n_s, d)
    sh1, sc1, g1, sh2, sc2, g2 = mod_s
    tm = n_s
    h = _norm_mod(xs, norm1_g[layer], sc1, sh1, tm, n_s)
    (q_s,) = _proj(h, w_in_bf, 0, D_ATTN, tm, 1024, rope=(cos_s, sin_s), scale=qk_scale, want_bf=True)
    k_s, k_sb = _proj(h, w_in_bf, D_ATTN, D_ATTN, tm, 1024, rope=(cos_s, sin_s), want_f32=True, want_bf=True)
    v_s, v_sb = _proj(h, w_in_bf, 2 * D_ATTN, D_ATTN, tm, 1024, want_f32=True, want_bf=True)
    u_s = _glu(h, w_in_bf, 3 * D_ATTN, 3 * D_ATTN + D_CONV, D_CONV, tm, 1024)
    qrep = jnp.tile(q_s.reshape(bs, ts, D_ATTN), (1, 2 * N_HEADS, 1))
    pad = ((0, 0), (0, PAGE_SIZE - ts), (0, 0))
    knew = jnp.pad(k_sb.reshape(bs, ts, D_ATTN), pad)
    vnew = jnp.pad(v_sb.reshape(bs, ts, D_ATTN), pad)
    n_pool = cache_k.shape[1]
    o_s = _sample_attn(page_table, qrep, knew, vnew,
                       cache_k[layer].reshape(n_pool, PAGE_SIZE, D_ATTN),
                       cache_v[layer].reshape(n_pool, PAGE_SIZE, D_ATTN),
                       lam_params, subln_g[layer], lam_init, ts)
    ext_s = jnp.concatenate([state_conv[layer], u_s.reshape(bs, ts, D_CONV)], axis=1)
    yc_s = _conv_sample(ext_s, conv_w[layer], conv_b[layer], conv_ln_g[layer], conv_ln_b[layer], ts)
    x1_s = _out_proj(o_s.reshape(n_s, D_ATTN).astype(BF16), yc_s.reshape(n_s, D_CONV).astype(BF16),
                     w_out_bf, xs, g1, tm, 1024, n_s)
    h2_s = _norm_mod(x1_s, norm2_g[layer], sc2, sh2, tm, n_s)
    y_sample = _peer_block(h2_s, x1_s, g2, pw, tm, n_s, n_s, n_s, 512).reshape(bs, ts, d)

    new_k_p = k_p.reshape(1, bp, tp, N_HEADS, 2, HEAD_DIM)
    new_v_p = v_p.reshape(1, bp, tp, N_HEADS, HEAD_W)
    new_conv_p = u_p3[:, tp - (CONV_WIDTH - 1):][None]
    new_k_s = k_s.reshape(1, bs, ts, N_HEADS, 2, HEAD_DIM)
    new_v_s = v_s.reshape(1, bs, ts, N_HEADS, HEAD_W)
    new_conv_s = ext_s[:, ts:][None]
    return (y_prompt, y_sample, new_k_p, new_v_p, new_conv_p, new_k_s, new_v_s, new_conv_s)
```

```python
import functools
import math

import jax
import jax.numpy as jnp
from jax import lax
from jax.experimental import pallas as pl
from jax.experimental.pallas import tpu as pltpu

F32 = jnp.float32
BF16 = jnp.bfloat16

D_MODEL = 4096
D_ATTN = 2048
D_CONV = 2048
HEAD_DIM = 64
HEAD_W = 2 * HEAD_DIM
N_HEADS = D_ATTN // HEAD_W
CONV_WIDTH = 31
CONV_HALO = 32
ROPE_THETA = 10000.0
PAGE_SIZE = 128
PEER_HEADS = 8
N_KEYS = 128
N_EXPERTS = N_KEYS * N_KEYS
PEER_TOPK = 16
D_QUERY = 256
EPS = 1e-6
NEG_INF = -1e30
LANES = 128
VMEM_LIMIT = 56 * 1024 * 1024


def _lambda_init(layer_idx):
    return 0.8 - 0.6 * math.exp(-0.3 * layer_idx)


def _cparams(*sem):
    return pltpu.CompilerParams(dimension_semantics=sem, vmem_limit_bytes=VMEM_LIMIT)


def _dot_nt(a, b):
    return lax.dot_general(a, b, (((1,), (1,)), ((), ())), preferred_element_type=F32)


def _dot_tn(a, b):
    return lax.dot_general(a, b, (((0,), (0,)), ((), ())), preferred_element_type=F32)


def _ada_kernel(c_ref, w_ref, b_ref, o_ref):
    c = c_ref[...]
    a = (c * jax.nn.sigmoid(c)).astype(BF16)
    o_ref[...] = jnp.dot(a, w_ref[...].astype(BF16), preferred_element_type=F32) + b_ref[...]


def _ada(c_all, w_ada, b_ada):
    m, d = c_all.shape
    n = w_ada.shape[1]
    tn = 1024
    return pl.pallas_call(
        _ada_kernel,
        grid=(n // tn,),
        in_specs=[pl.BlockSpec((m, d), lambda j: (0, 0)),
                  pl.BlockSpec((d, tn), lambda j: (0, j)),
                  pl.BlockSpec((1, tn), lambda j: (0, j))],
        out_specs=pl.BlockSpec((m, tn), lambda j: (0, j)),
        out_shape=jax.ShapeDtypeStruct((m, n), F32),
        compiler_params=_cparams("arbitrary"),
        name="ada",
    )(c_all, w_ada, b_ada.reshape(1, n))


def _mod_spec(mod3, tm, tn, rows_per_group, with_j):
    per_token = mod3.shape[1] != 1
    if per_token:
        if with_j:
            return pl.BlockSpec((None, tm, tn), lambda i, j: (0, i, j))
        return pl.BlockSpec((None, tm, tn), lambda i: (0, i, 0))
    tiles_per_group = rows_per_group // tm
    if with_j:
        return pl.BlockSpec((None, 1, tn), lambda i, j: (i // tiles_per_group, 0, j))
    return pl.BlockSpec((None, 1, tn), lambda i: (i // tiles_per_group, 0, 0))


def _rms(x):
    return x * lax.rsqrt(jnp.mean(x * x, axis=-1, keepdims=True) + EPS)


def _norm_mod_kernel(x_ref, g_ref, sc_ref, sh_ref, o_ref):
    y = _rms(x_ref[...]) * g_ref[...]
    o_ref[...] = (y * (1.0 + sc_ref[...]) + sh_ref[...]).astype(o_ref.dtype)


def _norm_mod(x, g, sc3, sh3, tm, rows_per_group):
    n, d = x.shape
    return pl.pallas_call(
        _norm_mod_kernel,
        grid=(n // tm,),
        in_specs=[pl.BlockSpec((tm, d), lambda i: (i, 0)),
                  pl.BlockSpec((1, d), lambda i: (0, 0)),
                  _mod_spec(sc3, tm, d, rows_per_group, False),
                  _mod_spec(sh3, tm, d, rows_per_group, False)],
        out_specs=pl.BlockSpec((tm, d), lambda i: (i, 0)),
        out_shape=jax.ShapeDtypeStruct((n, d), BF16),
        compiler_params=_cparams("arbitrary"),
        name="norm_mod",
    )(x, g.reshape(1, d), sc3, sh3)


def _rope_apply(z, cos_ref, sin_ref):
    cos, sin = cos_ref[...], sin_ref[...]
    lane = lax.broadcasted_iota(jnp.int32, cos.shape, 1)
    first = (lane & (HEAD_DIM - 1)) < HEAD_DIM // 2
    half = HEAD_DIM // 2
    outs = []
    for g in range(z.shape[1] // LANES):
        zg = z[:, g * LANES:(g + 1) * LANES]
        partner = jnp.where(first, pltpu.roll(zg, LANES - half, 1), pltpu.roll(zg, half, 1))
        outs.append(zg * cos + partner * sin)
    return jnp.concatenate(outs, axis=1)


def _proj_kernel(*refs, rope, scale, want_f32, want_bf, want_hm):
    h_ref, w_ref = refs[0], refs[1]
    pos = 2
    z = jnp.dot(h_ref[...], w_ref[...], preferred_element_type=F32)
    if rope:
        z = _rope_apply(z, refs[2], refs[3])
        pos = 4
    outs = refs[pos:]
    k = 0
    if want_f32:
        outs[k][...] = z
        k += 1
    zs = z * scale if scale != 1.0 else z
    if want_bf:
        outs[k][...] = zs.astype(BF16)
        k += 1
    if want_hm:
        for hh in range(z.shape[1] // HEAD_W):
            outs[k][hh] = zs[:, hh * HEAD_W:(hh + 1) * HEAD_W].astype(BF16)


def _proj(h, w_bf, col0, ncols, tm, tn, *, rope=None, scale=1.0, want_f32=False, want_bf=False,
          hm_seq=None):
    n, kdim = h.shape
    joff = col0 // tn
    in_specs = [pl.BlockSpec((tm, kdim), lambda i, j: (i, 0)),
                pl.BlockSpec((kdim, tn), lambda i, j: (0, j + joff))]
    args = [h, w_bf]
    if rope is not None:
        cos, sin = rope
        tiles = cos.shape[0] // tm
        in_specs += [pl.BlockSpec((tm, LANES), lambda i, j: (i % tiles, 0))] * 2
        args += [cos, sin]
    out_specs, out_shape = [], []
    if want_f32:
        out_specs.append(pl.BlockSpec((tm, tn), lambda i, j: (i, j)))
        out_shape.append(jax.ShapeDtypeStruct((n, ncols), F32))
    if want_bf:
        out_specs.append(pl.BlockSpec((tm, tn), lambda i, j: (i, j)))
        out_shape.append(jax.ShapeDtypeStruct((n, ncols), BF16))
    if hm_seq is not None:
        tps = hm_seq // tm
        hpt = tn // HEAD_W
        out_specs.append(pl.BlockSpec((None, hpt, tm, HEAD_W), lambda i, j: (i // tps, j, i % tps, 0)))
        out_shape.append(jax.ShapeDtypeStruct((n // hm_seq, ncols // HEAD_W, hm_seq, HEAD_W), BF16))
    kern = functools.partial(_proj_kernel, rope=rope is not None, scale=scale, want_f32=want_f32,
                             want_bf=want_bf, want_hm=hm_seq is not None)
    return pl.pallas_call(
        kern,
        grid=(n // tm, ncols // tn),
        in_specs=in_specs,
        out_specs=out_specs,
        out_shape=out_shape,
        compiler_params=_cparams("arbitrary", "arbitrary"),
        name="proj",
    )(*args)


def _glu_kernel(h_ref, wa_ref, wg_ref, o_ref):
    h = h_ref[...]
    a = jnp.dot(h, wa_ref[...], preferred_element_type=F32)
    g = jnp.dot(h, wg_ref[...], preferred_element_type=F32)
    o_ref[...] = a * jax.nn.sigmoid(g)


def _glu(h, w_bf, col_a, col_g, ncols, tm, tn):
    n, kdim = h.shape
    ja, jg = col_a // tn, col_g // tn
    return pl.pallas_call(
        _glu_kernel,
        grid=(n // tm, ncols // tn),
        in_specs=[pl.BlockSpec((tm, kdim), lambda i, j: (i, 0)),
                  pl.BlockSpec((kdim, tn), lambda i, j: (0, j + ja)),
                  pl.BlockSpec((kdim, tn), lambda i, j: (0, j + jg))],
        out_specs=pl.BlockSpec((tm, tn), lambda i, j: (i, j)),
        out_shape=jax.ShapeDtypeStruct((n, ncols), F32),
        compiler_params=_cparams("arbitrary", "arbitrary"),
        name="glu",
    )(h, w_bf, w_bf)


def _lam_value(lq1, lk1, lq2, lk2, lam_init):
    a = jnp.exp(jnp.sum(lq1[...] * lk1[...], axis=-1, keepdims=True))
    b = jnp.exp(jnp.sum(lq2[...] * lk2[...], axis=-1, keepdims=True))
    return a - b + lam_init


def _online_update(s, v, m, l, acc):
    m_new = jnp.maximum(m, jnp.max(s, axis=-1, keepdims=True))
    alpha = jnp.exp(m - m_new)
    p = jnp.exp(s - m_new)
    l_new = alpha * l + jnp.sum(p, axis=-1, keepdims=True)
    acc_new = alpha * acc + jnp.dot(p.astype(BF16), v, preferred_element_type=F32)
    return m_new, l_new, acc_new


def _prompt_attn_kernel(lq1, lk1, lq2, lk2, g_ref, q_ref, k_ref, v_ref, o_ref, *, tq, lam_init):
    qi = pl.program_id(2)
    lam = _lam_value(lq1, lk1, lq2, lk2, lam_init)
    q = q_ref[...]
    lane = lax.broadcasted_iota(jnp.int32, q.shape, 1)
    zero = jnp.zeros_like(q)
    q1 = jnp.where(lane < HEAD_DIM, q, zero)
    q2 = jnp.where(lane >= HEAD_DIM, q, zero)

    def step(j, carry, masked):
        m1, l1, a1, m2, l2, a2 = carry
        start = pl.multiple_of(j * tq, tq)
        kj = k_ref[pl.ds(start, tq), :]
        vj = v_ref[pl.ds(start, tq), :]
        s1 = _dot_nt(q1, kj)
        s2 = _dot_nt(q2, kj)
        if masked:
            row = lax.broadcasted_iota(jnp.int32, s1.shape, 0)
            col = lax.broadcasted_iota(jnp.int32, s1.shape, 1)
            keep = col <= row
            s1 = jnp.where(keep, s1, NEG_INF)
            s2 = jnp.where(keep, s2, NEG_INF)
        m1, l1, a1 = _online_update(s1, vj, m1, l1, a1)
        m2, l2, a2 = _online_update(s2, vj, m2, l2, a2)
        return m1, l1, a1, m2, l2, a2

    mi = jnp.full((tq, 1), NEG_INF, F32)
    li = jnp.zeros((tq, 1), F32)
    ai = jnp.zeros((tq, HEAD_W), F32)
    carry = lax.fori_loop(0, qi, lambda j, c: step(j, c, False), (mi, li, ai, mi, li, ai))
    m1, l1, a1, m2, l2, a2 = step(qi, carry, True)
    o = a1 / l1 - lam * (a2 / l2)
    o_ref[...] = (_rms(o) * g_ref[...] * (1.0 - lam_init)).astype(o_ref.dtype)


def _prompt_attn(q_hm, k_hm, v_hm, lam_params, subln_g, lam_init, tq):
    b, h, t, _ = q_hm.shape
    vec = pl.BlockSpec((1, HEAD_DIM), lambda bi, hi, qi: (0, 0))
    return pl.pallas_call(
        functools.partial(_prompt_attn_kernel, tq=tq, lam_init=lam_init),
        grid=(b, h, t // tq),
        in_specs=[vec, vec, vec, vec,
                  pl.BlockSpec((1, HEAD_W), lambda bi, hi, qi: (0, 0)),
                  pl.BlockSpec((None, None, tq, HEAD_W), lambda bi, hi, qi: (bi, hi, qi, 0)),
                  pl.BlockSpec((None, None, t, HEAD_W), lambda bi, hi, qi: (bi, hi, 0, 0)),
                  pl.BlockSpec((None, None, t, HEAD_W), lambda bi, hi, qi: (bi, hi, 0, 0))],
        out_specs=pl.BlockSpec((None, tq, HEAD_W), lambda bi, hi, qi: (bi, qi, hi)),
        out_shape=jax.ShapeDtypeStruct((b, t, h * HEAD_W), BF16),
        compiler_params=_cparams("arbitrary", "arbitrary", "arbitrary"),
        name="prompt_attn",
    )(*lam_params, subln_g.reshape(1, HEAD_W), q_hm, k_hm, v_hm)


ATT_TPAD = 8
ATT_HROWS = 2 * ATT_TPAD
ATT_GHEADS = 2


def _sample_attn_kernel(*refs, t_new, lam_init, n_pg):
    lq1, lk1, lq2, lk2, g_ref, qblk_ref, knew_ref, vnew_ref = refs[1:9]
    k_refs = refs[9:9 + n_pg]
    v_refs = refs[9 + n_pg:9 + 2 * n_pg]
    o_ref, m_s, l_s, acc_s = refs[9 + 2 * n_pg:]
    p = pl.program_id(1)
    gk = ATT_GHEADS * HEAD_W
    n_groups = N_HEADS // ATT_GHEADS

    def update(k_ref, v_ref, keep):
        s = jnp.concatenate(
            [jnp.dot(qblk_ref[g], k_ref[g * gk:(g + 1) * gk, :].astype(BF16), preferred_element_type=F32)
             for g in range(n_groups)], axis=0)
        if keep is not None:
            s = jnp.where(keep, s, NEG_INF)
        m_old = m_s[...]
        m_new = jnp.maximum(m_old, jnp.max(s, axis=-1, keepdims=True))
        alpha = jnp.exp(m_old - m_new)
        pr = jnp.exp(s - m_new)
        l_s[...] = alpha * l_s[...] + jnp.sum(pr, axis=-1, keepdims=True)
        m_s[...] = m_new
        pv = jnp.concatenate(
            [jnp.dot(pr[h * ATT_HROWS:(h + 1) * ATT_HROWS].astype(BF16),
                     v_ref[pl.ds(h, PAGE_SIZE, stride=N_HEADS), :].astype(BF16),
                     preferred_element_type=F32)
             for h in range(N_HEADS)], axis=0)
        acc_s[...] = alpha * acc_s[...] + pv

    @pl.when(p == 0)
    def _():
        m_s[...] = jnp.full(m_s.shape, NEG_INF, F32)
        l_s[...] = jnp.zeros(l_s.shape, F32)
        acc_s[...] = jnp.zeros(acc_s.shape, F32)
        r = lax.broadcasted_iota(jnp.int32, (acc_s.shape[0], PAGE_SIZE), 0)
        c = lax.broadcasted_iota(jnp.int32, (acc_s.shape[0], PAGE_SIZE), 1)
        keep = c <= jnp.minimum(r & (ATT_TPAD - 1), t_new - 1)
        update(knew_ref, vnew_ref, keep)

    for g in range(n_pg):
        update(k_refs[g], v_refs[g], None)

    @pl.when(p == pl.num_programs(1) - 1)
    def _():
        lam = _lam_value(lq1, lk1, lq2, lk2, lam_init)
        gain = g_ref[...]
        for h in range(N_HEADS):
            blk = acc_s[h * ATT_HROWS:(h + 1) * ATT_HROWS, :] / l_s[h * ATT_HROWS:(h + 1) * ATT_HROWS, :]
            o = blk[0:t_new] - lam * blk[ATT_TPAD:ATT_TPAD + t_new]
            o_ref[:, h * HEAD_W:(h + 1) * HEAD_W] = _rms(o) * gain * (1.0 - lam_init)


def _sample_attn(page_table, qblk, knew, vnew, cache_kt, cache_v2, lam_params, subln_g, lam_init, t_new, n_pg):
    bs, n_pages = page_table.shape
    rows = N_HEADS * ATT_HROWS
    kd = cache_kt.shape[1]
    vec = pl.BlockSpec((1, HEAD_DIM), lambda b, p, pt: (0, 0))

    def page_spec(g):
        return pl.BlockSpec((None, kd, PAGE_SIZE), lambda b, p, pt: (pt[b, p * n_pg + g], 0, 0))

    grid_spec = pltpu.PrefetchScalarGridSpec(
        num_scalar_prefetch=1,
        grid=(bs, n_pages // n_pg),
        in_specs=[vec, vec, vec, vec,
                  pl.BlockSpec((1, HEAD_W), lambda b, p, pt: (0, 0)),
                  pl.BlockSpec((None,) + qblk.shape[1:], lambda b, p, pt: (b, 0, 0, 0)),
                  pl.BlockSpec((None, kd, PAGE_SIZE), lambda b, p, pt: (b, 0, 0)),
                  pl.BlockSpec((None, kd, PAGE_SIZE), lambda b, p, pt: (b, 0, 0))]
                 + [page_spec(g) for g in range(n_pg)] * 2,
        out_specs=pl.BlockSpec((None, t_new, N_HEADS * HEAD_W), lambda b, p, pt: (b, 0, 0)),
        scratch_shapes=[pltpu.VMEM((rows, 1), F32),
                        pltpu.VMEM((rows, 1), F32),
                        pltpu.VMEM((rows, HEAD_W), F32)],
    )
    return pl.pallas_call(
        functools.partial(_sample_attn_kernel, t_new=t_new, lam_init=lam_init, n_pg=n_pg),
        grid_spec=grid_spec,
        out_shape=jax.ShapeDtypeStruct((bs, t_new, N_HEADS * HEAD_W), F32),
        compiler_params=_cparams("arbitrary", "arbitrary"),
        name="sample_attn",
    )(page_table, *lam_params, subln_g.reshape(1, HEAD_W), qblk, knew, vnew,
      *([cache_kt] * n_pg), *([cache_v2] * n_pg))


def _block_diag_queries(q, bs, ts):
    ng = N_HEADS // ATT_GHEADS
    nb = 2 * ATT_GHEADS
    x = q.reshape(bs, ts, ng, nb, HEAD_DIM).transpose(0, 2, 3, 1, 4)
    x = jnp.pad(x, ((0, 0), (0, 0), (0, 0), (0, ATT_TPAD - ts), (0, 0)))
    eye = jnp.eye(nb, dtype=q.dtype)[None, None, :, None, :, None]
    x = x[:, :, :, :, None, :] * eye
    return x.reshape(bs, ng, nb * ATT_TPAD, nb * HEAD_DIM)


def _ln_swish(y, lg_ref, lb_ref):
    mu = jnp.mean(y, axis=-1, keepdims=True)
    yc = y - mu
    z = yc * lax.rsqrt(jnp.mean(yc * yc, axis=-1, keepdims=True) + EPS) * lg_ref[...] + lb_ref[...]
    return z * jax.nn.sigmoid(z)


def _conv_prompt_kernel(halo_ref, u_ref, cw_ref, cb_ref, lg_ref, lb_ref, o_ref, ext_s, y_s, *, tt, cc):
    i = pl.program_id(1)
    ext_s[CONV_HALO:, :] = u_ref[...]

    @pl.when(i == 0)
    def _():
        ext_s[0:CONV_HALO, :] = jnp.zeros((CONV_HALO, ext_s.shape[1]), F32)

    @pl.when(i > 0)
    def _():
        ext_s[0:CONV_HALO, :] = halo_ref[...]

    first = CONV_HALO - (CONV_WIDTH - 1)

    def chunk(ci, carry):
        c0 = pl.multiple_of(ci * cc, cc)
        acc = jnp.zeros((tt, cc), F32)
        for w in range(CONV_WIDTH):
            acc = acc + ext_s[pl.ds(first + w, tt), pl.ds(c0, cc)] * cw_ref[w:w + 1, pl.ds(c0, cc)]
        y_s[:, pl.ds(c0, cc)] = acc + cb_ref[:, pl.ds(c0, cc)]
        return carry

    lax.fori_loop(0, ext_s.shape[1] // cc, chunk, 0)
    o_ref[...] = _ln_swish(y_s[...], lg_ref, lb_ref).astype(o_ref.dtype)


def _conv_prompt(u3, conv_w, conv_b, ln_g, ln_b, tt):
    b, t, c = u3.shape
    hb = tt // CONV_HALO
    row = pl.BlockSpec((1, c), lambda bi, i: (0, 0))
    return pl.pallas_call(
        functools.partial(_conv_prompt_kernel, tt=tt, cc=256),
        grid=(b, t // tt),
        in_specs=[pl.BlockSpec((None, CONV_HALO, c), lambda bi, i: (bi, jnp.maximum(i * hb - 1, 0), 0)),
                  pl.BlockSpec((None, tt, c), lambda bi, i: (bi, i, 0)),
                  pl.BlockSpec((CONV_WIDTH, c), lambda bi, i: (0, 0)),
                  row, row, row],
        out_specs=pl.BlockSpec((None, tt, c), lambda bi, i: (bi, i, 0)),
        out_shape=jax.ShapeDtypeStruct((b, t, c), BF16),
        scratch_shapes=[pltpu.VMEM((CONV_HALO + tt, c), F32), pltpu.VMEM((tt, c), F32)],
        compiler_params=_cparams("arbitrary", "arbitrary"),
        name="conv_prompt",
    )(u3, u3, conv_w, conv_b.reshape(1, c), ln_g.reshape(1, c), ln_b.reshape(1, c))


def _conv_sample_kernel(ext_ref, cw_ref, cb_ref, lg_ref, lb_ref, o_ref, *, t_new):
    acc = jnp.zeros((t_new, ext_ref.shape[1]), F32)
    for w in range(CONV_WIDTH):
        acc = acc + ext_ref[w:w + t_new, :] * cw_ref[w:w + 1, :]
    o_ref[...] = _ln_swish(acc + cb_ref[...], lg_ref, lb_ref)


def _conv_sample(ext3, conv_w, conv_b, ln_g, ln_b, t_new):
    b, rows, c = ext3.shape
    row = pl.BlockSpec((1, c), lambda bi: (0, 0))
    return pl.pallas_call(
        functools.partial(_conv_sample_kernel, t_new=t_new),
        grid=(b,),
        in_specs=[pl.BlockSpec((None, rows, c), lambda bi: (bi, 0, 0)),
                  pl.BlockSpec((CONV_WIDTH, c), lambda bi: (0, 0)),
                  row, row, row],
        out_specs=pl.BlockSpec((None, t_new, c), lambda bi: (bi, 0, 0)),
        out_shape=jax.ShapeDtypeStruct((b, t_new, c), F32),
        compiler_params=_cparams("arbitrary"),
        name="conv_sample",
    )(ext3, conv_w, conv_b.reshape(1, c), ln_g.reshape(1, c), ln_b.reshape(1, c))


def _out_kernel(o_ref, yc_ref, wt_ref, wb_ref, x_ref, g_ref, out_ref):
    mix = (jnp.dot(o_ref[...], wt_ref[...], preferred_element_type=F32)
           + jnp.dot(yc_ref[...], wb_ref[...], preferred_element_type=F32))
    out_ref[...] = x_ref[...] + g_ref[...] * mix


def _out_proj(o_bf, yc_bf, w_out_bf, x, g3, tm, tn, rows_per_group):
    n, ka = o_bf.shape
    kc = yc_bf.shape[1]
    d = w_out_bf.shape[1]
    assert ka == kc
    return pl.pallas_call(
        _out_kernel,
        grid=(n // tm, d // tn),
        in_specs=[pl.BlockSpec((tm, ka), lambda i, j: (i, 0)),
                  pl.BlockSpec((tm, kc), lambda i, j: (i, 0)),
                  pl.BlockSpec((ka, tn), lambda i, j: (0, j)),
                  pl.BlockSpec((kc, tn), lambda i, j: (1, j)),
                  pl.BlockSpec((tm, tn), lambda i, j: (i, j)),
                  _mod_spec(g3, tm, tn, rows_per_group, True)],
        out_specs=pl.BlockSpec((tm, tn), lambda i, j: (i, j)),
        out_shape=jax.ShapeDtypeStruct((n, d), F32),
        compiler_params=_cparams("arbitrary", "arbitrary"),
        name="out_proj",
    )(o_bf, yc_bf, w_out_bf, w_out_bf, x, g3)


def _drop_first_max(work, rid, fill):
    m = jnp.max(work, axis=0, keepdims=True)
    first = jnp.min(jnp.where(work == m, rid, float(work.shape[0])), axis=0, keepdims=True)
    return m, jnp.where(rid == first, fill, work)


def _row_ids(shape):
    return lax.broadcasted_iota(jnp.int32, shape, 0).astype(F32)


def _top_rows(s, k):
    rid = _row_ids(s.shape)
    rows = []
    work = s
    for _ in range(k):
        m, work = _drop_first_max(work, rid, -jnp.inf)
        rows.append(m)
    return jnp.concatenate(rows, axis=0)


def _kth_largest(c, k):
    rid = _row_ids(c.shape)
    work = c
    for _ in range(k - 1):
        _, work = _drop_first_max(work, rid, -1.0)
    return jnp.max(work, axis=0, keepdims=True)


def _candidates(pa, pb):
    parts = [pa[0:1] * pb]
    for a in range(1, 8):
        parts.append(pa[a:a + 1] * pb[0:8])
    parts.append(pa[8:16] * pb[0:1])
    return jnp.concatenate(parts, axis=0)


def _peer_route_kernel(q_ref, keys_ref, ea_ref, eb_ref, pt_ref):
    q = q_ref[...].astype(BF16)
    half = D_QUERY // 2
    pts = []
    for h in range(PEER_HEADS):
        s1 = _dot_nt(keys_ref[h, 0], q[:, (2 * h) * half:(2 * h + 1) * half])
        s2 = _dot_nt(keys_ref[h, 1], q[:, (2 * h + 1) * half:(2 * h + 2) * half])
        sv1 = _top_rows(s1, PEER_TOPK)
        sv2 = _top_rows(s2, PEER_TOPK)
        m1, m2 = sv1[0:1], sv2[0:1]
        ea = jnp.exp(s1 - m1)
        eb = jnp.exp(s2 - m2)
        pa = jnp.exp(sv1 - m1)
        pb = jnp.exp(sv2 - m2)
        cand = _candidates(pa, pb)
        sel = cand >= _kth_largest(cand, PEER_TOPK)
        zinv = 1.0 / jnp.sum(jnp.where(sel, cand, 0.0), axis=0, keepdims=True)
        cand_n = _candidates(pa, pb * zinv)
        pts.append(jnp.min(jnp.where(sel, cand_n, jnp.inf), axis=0, keepdims=True))
        ea_ref[h] = ea
        eb_ref[h] = eb * zinv
    pt_ref[...] = jnp.concatenate(pts, axis=0)


def _peer_route(q, keys_bf, tt):
    n, dq = q.shape
    return pl.pallas_call(
        _peer_route_kernel,
        grid=(n // tt,),
        in_specs=[pl.BlockSpec((tt, dq), lambda i: (i, 0)),
                  pl.BlockSpec(keys_bf.shape, lambda i: (0, 0, 0, 0))],
        out_specs=[pl.BlockSpec((PEER_HEADS, N_KEYS, tt), lambda i: (0, 0, i)),
                   pl.BlockSpec((PEER_HEADS, N_KEYS, tt), lambda i: (0, 0, i)),
                   pl.BlockSpec((PEER_HEADS, tt), lambda i: (0, i))],
        out_shape=[jax.ShapeDtypeStruct((PEER_HEADS, N_KEYS, n), F32),
                   jax.ShapeDtypeStruct((PEER_HEADS, N_KEYS, n), F32),
                   jax.ShapeDtypeStruct((PEER_HEADS, n), F32)],
        compiler_params=_cparams("arbitrary"),
        name="peer_route",
    )(q, keys_bf)


def _gelu_exact(x):
    return 0.5 * x * (1.0 + lax.erf(x * (2.0 ** -0.5)))


def _peer_dense_kernel(h_ref, u_ref, v_ref, ea_ref, eb_ref, pt_ref, y_ref, *, ec):
    c = pl.program_id(1)

    @pl.when(c == 0)
    def _():
        y_ref[...] = jnp.zeros(y_ref.shape, F32)

    act = _gelu_exact(_dot_nt(u_ref[...], h_ref[...]))
    groups = ec // N_KEYS
    pieces = []
    for ii in range(groups):
        i = c * groups + ii
        gate = jnp.zeros((N_KEYS, act.shape[1]), F32)
        for h in range(PEER_HEADS):
            p = ea_ref[h, pl.ds(i, 1), :] * eb_ref[h]
            gate = gate + jnp.where(p >= pt_ref[h:h + 1, :], p, 0.0)
        pieces.append((gate * act[ii * N_KEYS:(ii + 1) * N_KEYS]).astype(BF16))
    coef = jnp.concatenate(pieces, axis=0)
    y_ref[...] += _dot_tn(coef, v_ref[...])


def _peer_dense(h2, u_bf, v_bf, ea, eb, pt, tt, ec):
    n, d = h2.shape
    ne = u_bf.shape[0]
    return pl.pallas_call(
        functools.partial(_peer_dense_kernel, ec=ec),
        grid=(n // tt, ne // ec),
        in_specs=[pl.BlockSpec((tt, d), lambda i, c: (i, 0)),
                  pl.BlockSpec((ec, d), lambda i, c: (c, 0)),
                  pl.BlockSpec((ec, d), lambda i, c: (c, 0)),
                  pl.BlockSpec((PEER_HEADS, N_KEYS, tt), lambda i, c: (0, 0, i)),
                  pl.BlockSpec((PEER_HEADS, N_KEYS, tt), lambda i, c: (0, 0, i)),
                  pl.BlockSpec((PEER_HEADS, tt), lambda i, c: (0, i))],
        out_specs=pl.BlockSpec((tt, d), lambda i, c: (i, 0)),
        out_shape=jax.ShapeDtypeStruct((n, d), F32),
        compiler_params=_cparams("arbitrary", "arbitrary"),
        name="peer_dense",
    )(h2, u_bf, v_bf, ea, eb, pt)


def _final_kernel(x_ref, y_ref, g2_ref, gf_ref, o_ref):
    x = x_ref[...] + g2_ref[...] * y_ref[...]
    o_ref[...] = _rms(x) * gf_ref[...]


def _final(x1, y, g23, normf_g, tm, rows_per_group):
    n, d = x1.shape
    return pl.pallas_call(
        _final_kernel,
        grid=(n // tm,),
        in_specs=[pl.BlockSpec((tm, d), lambda i: (i, 0)),
                  pl.BlockSpec((tm, d), lambda i: (i, 0)),
                  _mod_spec(g23, tm, d, rows_per_group, False),
                  pl.BlockSpec((1, d), lambda i: (0, 0))],
        out_specs=pl.BlockSpec((tm, d), lambda i: (i, 0)),
        out_shape=jax.ShapeDtypeStruct((n, d), F32),
        compiler_params=_cparams("arbitrary"),
        name="final",
    )(x1, y, g23, normf_g.reshape(1, d))


def _rope_tables(pos):
    inv = ROPE_THETA ** (-jnp.arange(0, HEAD_DIM, 2, dtype=F32) / HEAD_DIM)
    ang = pos.astype(F32)[:, None] * inv[None, :]
    cos, sin = jnp.cos(ang), jnp.sin(ang)
    cos_t = jnp.concatenate([cos, cos, cos, cos], axis=1)
    sin_t = jnp.concatenate([-sin, sin, -sin, sin], axis=1)
    return cos_t, sin_t


def _peer_block(h2, x1, g23, w, tm_rows, rows_per_group, tt_route, tt_dense, ec):
    q = _proj(h2, w["wq"], 0, w["wq"].shape[1], tm_rows, 512, want_f32=True)[0]
    ea, eb, pt = _peer_route(q, w["keys"], tt_route)
    y = _peer_dense(h2, w["u"], w["v"], ea, eb, pt, tt_dense, ec)
    return _final(x1, y, g23, w["normf_g"], min(tm_rows, 256), rows_per_group)


def kernel(x_prompt, x_sample, c_prompt, c_sample, cache_k, cache_v, state_conv, page_table, w_ada, b_ada,
           norm1_g, norm2_g, w_in, lambda_q1, lambda_k1, lambda_q2, lambda_k2, subln_g, conv_w, conv_b,
           conv_ln_g, conv_ln_b, w_out, peer_wq, peer_keys, peer_u, peer_v, normf_g):
    depth = w_ada.shape[0]
    assert depth == 1, "single-layer trunk"
    layer = 0
    lam_init = _lambda_init(layer)
    bp, tp, d = x_prompt.shape
    bs, ts, _ = x_sample.shape
    n_p, n_s = bp * tp, bs * ts
    past = page_table.shape[1] * PAGE_SIZE

    w_in_bf = w_in[layer].astype(BF16)
    w_out_bf = w_out[layer].astype(BF16)
    pw = dict(wq=peer_wq[layer].astype(BF16), keys=peer_keys[layer].astype(BF16),
              u=peer_u[layer].astype(BF16), v=peer_v[layer].astype(BF16), normf_g=normf_g)
    lam_params = [p[layer].reshape(1, HEAD_DIM) for p in (lambda_q1, lambda_k1, lambda_q2, lambda_k2)]

    mod = _ada(jnp.concatenate([c_prompt, c_sample], axis=0), w_ada[layer], b_ada[layer])
    mod_p = [m.reshape(bp, 1, d) for m in jnp.split(mod[:bp], 6, axis=-1)]
    mod_s = [jnp.repeat(m, ts, axis=0).reshape(1, n_s, d) for m in jnp.split(mod[bp:], 6, axis=-1)]

    cos_p, sin_p = _rope_tables(jnp.arange(tp, dtype=jnp.int32))
    pos_s = past + jnp.arange(ts, dtype=jnp.int32)
    cos_s, sin_s = _rope_tables(jnp.tile(pos_s, bs))
    qk_scale = HEAD_DIM ** -0.5

    xp = x_prompt.reshape(n_p, d)
    sh1, sc1, g1, sh2, sc2, g2 = mod_p
    tm = 1024
    h = _norm_mod(xp, norm1_g[layer], sc1, sh1, 256, tp)
    (q_hm,) = _proj(h, w_in_bf, 0, D_ATTN, tm, 512, rope=(cos_p, sin_p), scale=qk_scale, hm_seq=tp)
    k_p, k_hm = _proj(h, w_in_bf, D_ATTN, D_ATTN, tm, 512, rope=(cos_p, sin_p), want_f32=True, hm_seq=tp)
    v_p, v_hm = _proj(h, w_in_bf, 2 * D_ATTN, D_ATTN, tm, 512, want_f32=True, hm_seq=tp)
    u_p = _glu(h, w_in_bf, 3 * D_ATTN, 3 * D_ATTN + D_CONV, D_CONV, tm, 512)
    o_p = _prompt_attn(q_hm, k_hm, v_hm, lam_params, subln_g[layer], lam_init, 512)
    u_p3 = u_p.reshape(bp, tp, D_CONV)
    yc_p = _conv_prompt(u_p3, conv_w[layer], conv_b[layer], conv_ln_g[layer], conv_ln_b[layer], 128)
    x1_p = _out_proj(o_p.reshape(n_p, D_ATTN), yc_p.reshape(n_p, D_CONV), w_out_bf, xp, g1, tm, 512, tp)
    h2_p = _norm_mod(x1_p, norm2_g[layer], sc2, sh2, 256, tp)
    y_prompt = _peer_block(h2_p, x1_p, g2, pw, tm, tp, 256, 512, 512).reshape(bp, tp, d)

    xs = x_sample.reshape(n_s, d)
    sh1, sc1, g1, sh2, sc2, g2 = mod_s
    tm = n_s
    h = _norm_mod(xs, norm1_g[layer], sc1, sh1, tm, n_s)
    (q_s,) = _proj(h, w_in_bf, 0, D_ATTN, tm, 1024, rope=(cos_s, sin_s), scale=qk_scale, want_bf=True)
    (k_s,) = _proj(h, w_in_bf, D_ATTN, D_ATTN, tm, 1024, rope=(cos_s, sin_s), want_f32=True)
    (v_s,) = _proj(h, w_in_bf, 2 * D_ATTN, D_ATTN, tm, 1024, want_f32=True)
    u_s = _glu(h, w_in_bf, 3 * D_ATTN, 3 * D_ATTN + D_CONV, D_CONV, tm, 1024)
    qblk = _block_diag_queries(q_s, bs, ts)
    knew = jnp.pad(k_s.reshape(bs, ts, D_ATTN).transpose(0, 2, 1), ((0, 0), (0, 0), (0, PAGE_SIZE - ts)))
    vnew = jnp.pad(v_s.reshape(bs, ts * N_HEADS, HEAD_W), ((0, 0), (0, (PAGE_SIZE - ts) * N_HEADS), (0, 0)))
    n_pool = cache_k.shape[1]
    cache_kt = cache_k[layer].transpose(0, 2, 3, 4, 1).reshape(n_pool, D_ATTN, PAGE_SIZE)
    cache_v2 = cache_v[layer].reshape(n_pool, PAGE_SIZE * N_HEADS, HEAD_W)
    o_s = _sample_attn(page_table, qblk, knew, vnew, cache_kt, cache_v2,
                       lam_params, subln_g[layer], lam_init, ts, 8)
    ext_s = jnp.concatenate([state_conv[layer], u_s.reshape(bs, ts, D_CONV)], axis=1)
    yc_s = _conv_sample(ext_s, conv_w[layer], conv_b[layer], conv_ln_g[layer], conv_ln_b[layer], ts)
    x1_s = _out_proj(o_s.reshape(n_s, D_ATTN).astype(BF16), yc_s.reshape(n_s, D_CONV).astype(BF16),
                     w_out_bf, xs, g1, tm, 1024, n_s)
    h2_s = _norm_mod(x1_s, norm2_g[layer], sc2, sh2, tm, n_s)
    y_sample = _peer_block(h2_s, x1_s, g2, pw, tm, n_s, n_s, n_s, 512).reshape(bs, ts, d)

    new_k_p = k_p.reshape(1, bp, tp, N_HEADS, 2, HEAD_DIM)
    new_v_p = v_p.reshape(1, bp, tp, N_HEADS, HEAD_W)
    new_conv_p = u_p3[:, tp - (CONV_WIDTH - 1):][None]
    new_k_s = k_s.reshape(1, bs, ts, N_HEADS, 2, HEAD_DIM)
    new_v_s = v_s.reshape(1, bs, ts, N_HEADS, HEAD_W)
    new_conv_s = ext_s[:, ts:][None]
    return (y_prompt, y_sample, new_k_p, new_v_p, new_conv_p, new_k_s, new_v_s, new_conv_s)
```

```python
import functools
import math

import jax
import jax.numpy as jnp
from jax import lax
from jax.experimental import pallas as pl
from jax.experimental.pallas import tpu as pltpu

F32 = jnp.float32
BF16 = jnp.bfloat16

D_MODEL = 4096
D_ATTN = 2048
D_CONV = 2048
HEAD_DIM = 64
HEAD_W = 2 * HEAD_DIM
N_HEADS = D_ATTN // HEAD_W
CONV_WIDTH = 31
CONV_HALO = 32
ROPE_THETA = 10000.0
PAGE_SIZE = 128
PEER_HEADS = 8
N_KEYS = 128
N_EXPERTS = N_KEYS * N_KEYS
PEER_TOPK = 16
D_QUERY = 256
EPS = 1e-6
NEG_INF = -1e30
LANES = 128
SUBLANES = 8
VMEM_LIMIT = 56 * 1024 * 1024


def _lambda_init(layer_idx):
    return 0.8 - 0.6 * math.exp(-0.3 * layer_idx)


def _cparams(*sem):
    return pltpu.CompilerParams(dimension_semantics=sem, vmem_limit_bytes=VMEM_LIMIT)


def _dot_nt(a, b):
    return lax.dot_general(a, b, (((1,), (1,)), ((), ())), preferred_element_type=F32)


def _dot_tn(a, b):
    return lax.dot_general(a, b, (((0,), (0,)), ((), ())), preferred_element_type=F32)


def _ada_kernel(c_ref, w_ref, b_ref, o_ref):
    c = c_ref[...]
    a = (c * jax.nn.sigmoid(c)).astype(BF16)
    o_ref[...] = jnp.dot(a, w_ref[...].astype(BF16), preferred_element_type=F32) + b_ref[...]


def _ada(c_all, w_ada, b_ada):
    m, d = c_all.shape
    n = w_ada.shape[1]
    tn = 1024
    return pl.pallas_call(
        _ada_kernel,
        grid=(n // tn,),
        in_specs=[pl.BlockSpec((m, d), lambda j: (0, 0)),
                  pl.BlockSpec((d, tn), lambda j: (0, j)),
                  pl.BlockSpec((1, tn), lambda j: (0, j))],
        out_specs=pl.BlockSpec((m, tn), lambda j: (0, j)),
        out_shape=jax.ShapeDtypeStruct((m, n), F32),
        compiler_params=_cparams("arbitrary"),
        name="ada",
    )(c_all, w_ada, b_ada.reshape(1, n))


def _mod_spec(mod3, tm, tn, rows_per_group, with_j):
    per_token = mod3.shape[1] != 1
    if per_token:
        if with_j:
            return pl.BlockSpec((None, tm, tn), lambda i, j: (0, i, j))
        return pl.BlockSpec((None, tm, tn), lambda i: (0, i, 0))
    tiles_per_group = rows_per_group // tm
    if with_j:
        return pl.BlockSpec((None, 1, tn), lambda i, j: (i // tiles_per_group, 0, j))
    return pl.BlockSpec((None, 1, tn), lambda i: (i // tiles_per_group, 0, 0))


def _rms(x):
    return x * lax.rsqrt(jnp.mean(x * x, axis=-1, keepdims=True) + EPS)


def _norm_mod_kernel(x_ref, g_ref, sc_ref, sh_ref, o_ref):
    y = _rms(x_ref[...]) * g_ref[...]
    o_ref[...] = (y * (1.0 + sc_ref[...]) + sh_ref[...]).astype(o_ref.dtype)


def _norm_mod(x, g, sc3, sh3, tm, rows_per_group):
    n, d = x.shape
    return pl.pallas_call(
        _norm_mod_kernel,
        grid=(n // tm,),
        in_specs=[pl.BlockSpec((tm, d), lambda i: (i, 0)),
                  pl.BlockSpec((1, d), lambda i: (0, 0)),
                  _mod_spec(sc3, tm, d, rows_per_group, False),
                  _mod_spec(sh3, tm, d, rows_per_group, False)],
        out_specs=pl.BlockSpec((tm, d), lambda i: (i, 0)),
        out_shape=jax.ShapeDtypeStruct((n, d), BF16),
        compiler_params=_cparams("arbitrary"),
        name="norm_mod",
    )(x, g.reshape(1, d), sc3, sh3)


def _rope_apply(z, cos_ref, sin_ref):
    cos, sin = cos_ref[...], sin_ref[...]
    lane = lax.broadcasted_iota(jnp.int32, cos.shape, 1)
    first = (lane & (HEAD_DIM - 1)) < HEAD_DIM // 2
    half = HEAD_DIM // 2
    outs = []
    for g in range(z.shape[1] // LANES):
        zg = z[:, g * LANES:(g + 1) * LANES]
        partner = jnp.where(first, pltpu.roll(zg, LANES - half, 1), pltpu.roll(zg, half, 1))
        outs.append(zg * cos + partner * sin)
    return jnp.concatenate(outs, axis=1)


def _proj_kernel(*refs, rope, scale, want_f32, want_bf, want_hm):
    h_ref, w_ref = refs[0], refs[1]
    pos = 2
    z = jnp.dot(h_ref[...], w_ref[...], preferred_element_type=F32)
    if rope:
        z = _rope_apply(z, refs[2], refs[3])
        pos = 4
    outs = refs[pos:]
    k = 0
    if want_f32:
        outs[k][...] = z
        k += 1
    zs = z * scale if scale != 1.0 else z
    if want_bf:
        outs[k][...] = zs.astype(BF16)
        k += 1
    if want_hm:
        for hh in range(z.shape[1] // HEAD_W):
            outs[k][hh] = zs[:, hh * HEAD_W:(hh + 1) * HEAD_W].astype(BF16)


def _proj(h, w_bf, col0, ncols, tm, tn, *, rope=None, scale=1.0, want_f32=False, want_bf=False,
          hm_seq=None):
    n, kdim = h.shape
    joff = col0 // tn
    in_specs = [pl.BlockSpec((tm, kdim), lambda i, j: (i, 0)),
                pl.BlockSpec((kdim, tn), lambda i, j: (0, j + joff))]
    args = [h, w_bf]
    if rope is not None:
        cos, sin = rope
        tiles = cos.shape[0] // tm
        in_specs += [pl.BlockSpec((tm, LANES), lambda i, j: (i % tiles, 0))] * 2
        args += [cos, sin]
    out_specs, out_shape = [], []
    if want_f32:
        out_specs.append(pl.BlockSpec((tm, tn), lambda i, j: (i, j)))
        out_shape.append(jax.ShapeDtypeStruct((n, ncols), F32))
    if want_bf:
        out_specs.append(pl.BlockSpec((tm, tn), lambda i, j: (i, j)))
        out_shape.append(jax.ShapeDtypeStruct((n, ncols), BF16))
    if hm_seq is not None:
        tps = hm_seq // tm
        hpt = tn // HEAD_W
        out_specs.append(pl.BlockSpec((None, hpt, tm, HEAD_W), lambda i, j: (i // tps, j, i % tps, 0)))
        out_shape.append(jax.ShapeDtypeStruct((n // hm_seq, ncols // HEAD_W, hm_seq, HEAD_W), BF16))
    kern = functools.partial(_proj_kernel, rope=rope is not None, scale=scale, want_f32=want_f32,
                             want_bf=want_bf, want_hm=hm_seq is not None)
    return pl.pallas_call(
        kern,
        grid=(n // tm, ncols // tn),
        in_specs=in_specs,
        out_specs=out_specs,
        out_shape=out_shape,
        compiler_params=_cparams("arbitrary", "arbitrary"),
        name="proj",
    )(*args)


def _glu_kernel(h_ref, wa_ref, wg_ref, o_ref):
    h = h_ref[...]
    a = jnp.dot(h, wa_ref[...], preferred_element_type=F32)
    g = jnp.dot(h, wg_ref[...], preferred_element_type=F32)
    o_ref[...] = a * jax.nn.sigmoid(g)


def _glu(h, w_bf, col_a, col_g, ncols, tm, tn):
    n, kdim = h.shape
    ja, jg = col_a // tn, col_g // tn
    return pl.pallas_call(
        _glu_kernel,
        grid=(n // tm, ncols // tn),
        in_specs=[pl.BlockSpec((tm, kdim), lambda i, j: (i, 0)),
                  pl.BlockSpec((kdim, tn), lambda i, j: (0, j + ja)),
                  pl.BlockSpec((kdim, tn), lambda i, j: (0, j + jg))],
        out_specs=pl.BlockSpec((tm, tn), lambda i, j: (i, j)),
        out_shape=jax.ShapeDtypeStruct((n, ncols), F32),
        compiler_params=_cparams("arbitrary", "arbitrary"),
        name="glu",
    )(h, w_bf, w_bf)


def _lam_value(lq1, lk1, lq2, lk2, lam_init):
    a = jnp.exp(jnp.sum(lq1[...] * lk1[...], axis=-1, keepdims=True))
    b = jnp.exp(jnp.sum(lq2[...] * lk2[...], axis=-1, keepdims=True))
    return a - b + lam_init


def _prompt_attn_kernel(lq1, lk1, lq2, lk2, g_ref, q_ref, k_ref, v_ref, o_ref, qs_s, m_s, l_s, acc_s, *,
                        tq, lam_init):
    qi = pl.program_id(2)
    q = q_ref[...]
    lane = lax.broadcasted_iota(jnp.int32, q.shape, 1)
    zero = jnp.zeros_like(q)
    qs_s[0:tq, :] = jnp.where(lane < HEAD_DIM, q, zero)
    qs_s[tq:2 * tq, :] = jnp.where(lane >= HEAD_DIM, q, zero)
    m_s[...] = jnp.full(m_s.shape, NEG_INF, F32)
    l_s[...] = jnp.zeros(l_s.shape, F32)
    acc_s[...] = jnp.zeros(acc_s.shape, F32)

    def step(j, masked):
        start = pl.multiple_of(j * tq, tq)
        kj = k_ref[pl.ds(start, tq), :]
        vj = v_ref[pl.ds(start, tq), :]
        s = _dot_nt(qs_s[...], kj)
        if masked:
            row = lax.broadcasted_iota(jnp.int32, s.shape, 0)
            row = jnp.where(row >= tq, row - tq, row)
            col = lax.broadcasted_iota(jnp.int32, s.shape, 1)
            s = jnp.where(col <= row, s, NEG_INF)
        m_old = m_s[...]
        m_new = jnp.maximum(m_old, jnp.max(s, axis=-1, keepdims=True))
        alpha = jnp.exp(m_old - m_new)
        p = jnp.exp(s - pltpu.repeat(m_new, tq // LANES, 1))
        l_s[...] = alpha * l_s[...] + jnp.sum(p, axis=-1, keepdims=True)
        m_s[...] = m_new
        acc_s[...] = alpha * acc_s[...] + jnp.dot(p.astype(BF16), vj, preferred_element_type=F32)

    def body(j, carry):
        step(j, False)
        return carry

    lax.fori_loop(0, qi, body, 0)
    step(qi, True)
    lam = _lam_value(lq1, lk1, lq2, lk2, lam_init)
    o = acc_s[0:tq] / l_s[0:tq] - lam * (acc_s[tq:2 * tq] / l_s[tq:2 * tq])
    o_ref[...] = (_rms(o) * g_ref[...] * (1.0 - lam_init)).astype(o_ref.dtype)


def _prompt_attn(q_hm, k_hm, v_hm, lam_params, subln_g, lam_init, tq):
    b, h, t, _ = q_hm.shape
    vec = pl.BlockSpec((1, HEAD_DIM), lambda bi, hi, qi: (0, 0))
    return pl.pallas_call(
        functools.partial(_prompt_attn_kernel, tq=tq, lam_init=lam_init),
        grid=(b, h, t // tq),
        in_specs=[vec, vec, vec, vec,
                  pl.BlockSpec((1, HEAD_W), lambda bi, hi, qi: (0, 0)),
                  pl.BlockSpec((None, None, tq, HEAD_W), lambda bi, hi, qi: (bi, hi, qi, 0)),
                  pl.BlockSpec((None, None, t, HEAD_W), lambda bi, hi, qi: (bi, hi, 0, 0)),
                  pl.BlockSpec((None, None, t, HEAD_W), lambda bi, hi, qi: (bi, hi, 0, 0))],
        out_specs=pl.BlockSpec((None, tq, HEAD_W), lambda bi, hi, qi: (bi, qi, hi)),
        out_shape=jax.ShapeDtypeStruct((b, t, h * HEAD_W), BF16),
        scratch_shapes=[pltpu.VMEM((2 * tq, HEAD_W), BF16), pltpu.VMEM((2 * tq, LANES), F32),
                        pltpu.VMEM((2 * tq, LANES), F32), pltpu.VMEM((2 * tq, HEAD_W), F32)],
        compiler_params=_cparams("arbitrary", "arbitrary", "arbitrary"),
        name="prompt_attn",
    )(*lam_params, subln_g.reshape(1, HEAD_W), q_hm, k_hm, v_hm)


ATT_TPAD = 8
ATT_HROWS = 2 * ATT_TPAD
ATT_GHEADS = 2


def _sample_attn_kernel(*refs, t_new, lam_init, n_pg):
    lq1, lk1, lq2, lk2, g_ref, qblk_ref, knew_ref, vnew_ref = refs[1:9]
    k_refs = refs[9:9 + n_pg]
    v_refs = refs[9 + n_pg:9 + 2 * n_pg]
    o_ref, m_s, l_s, acc_s = refs[9 + 2 * n_pg:]
    p = pl.program_id(1)
    gk = ATT_GHEADS * HEAD_W
    n_groups = N_HEADS // ATT_GHEADS

    def update(k_ref, v_ref, keep):
        s = jnp.concatenate(
            [jnp.dot(qblk_ref[g], k_ref[g * gk:(g + 1) * gk, :].astype(BF16), preferred_element_type=F32)
             for g in range(n_groups)], axis=0)
        if keep is not None:
            s = jnp.where(keep, s, NEG_INF)
        m_old = m_s[...]
        m_new = jnp.maximum(m_old, jnp.max(s, axis=-1, keepdims=True))
        alpha = jnp.exp(m_old - m_new)
        pr = jnp.exp(s - m_new)
        l_s[...] = alpha * l_s[...] + jnp.sum(pr, axis=-1, keepdims=True)
        m_s[...] = m_new
        pv = jnp.concatenate(
            [jnp.dot(pr[h * ATT_HROWS:(h + 1) * ATT_HROWS].astype(BF16),
                     v_ref[pl.ds(h, PAGE_SIZE, stride=N_HEADS), :].astype(BF16),
                     preferred_element_type=F32)
             for h in range(N_HEADS)], axis=0)
        acc_s[...] = alpha * acc_s[...] + pv

    @pl.when(p == 0)
    def _():
        m_s[...] = jnp.full(m_s.shape, NEG_INF, F32)
        l_s[...] = jnp.zeros(l_s.shape, F32)
        acc_s[...] = jnp.zeros(acc_s.shape, F32)
        r = lax.broadcasted_iota(jnp.int32, (acc_s.shape[0], PAGE_SIZE), 0)
        c = lax.broadcasted_iota(jnp.int32, (acc_s.shape[0], PAGE_SIZE), 1)
        keep = c <= jnp.minimum(r & (ATT_TPAD - 1), t_new - 1)
        update(knew_ref, vnew_ref, keep)

    for g in range(n_pg):
        update(k_refs[g], v_refs[g], None)

    @pl.when(p == pl.num_programs(1) - 1)
    def _():
        lam = _lam_value(lq1, lk1, lq2, lk2, lam_init)
        gain = g_ref[...]
        for h in range(N_HEADS):
            blk = acc_s[h * ATT_HROWS:(h + 1) * ATT_HROWS, :] / l_s[h * ATT_HROWS:(h + 1) * ATT_HROWS, :]
            o = blk[0:t_new] - lam * blk[ATT_TPAD:ATT_TPAD + t_new]
            o_ref[:, h * HEAD_W:(h + 1) * HEAD_W] = _rms(o) * gain * (1.0 - lam_init)


def _sample_attn(page_table, qblk, knew, vnew, cache_kt, cache_v2, lam_params, subln_g, lam_init, t_new, n_pg):
    bs, n_pages = page_table.shape
    rows = N_HEADS * ATT_HROWS
    kd = cache_kt.shape[1]
    vec = pl.BlockSpec((1, HEAD_DIM), lambda b, p, pt: (0, 0))

    def page_spec(g):
        return pl.BlockSpec((None, kd, PAGE_SIZE), lambda b, p, pt: (pt[b, p * n_pg + g], 0, 0))

    grid_spec = pltpu.PrefetchScalarGridSpec(
        num_scalar_prefetch=1,
        grid=(bs, n_pages // n_pg),
        in_specs=[vec, vec, vec, vec,
                  pl.BlockSpec((1, HEAD_W), lambda b, p, pt: (0, 0)),
                  pl.BlockSpec((None,) + qblk.shape[1:], lambda b, p, pt: (b, 0, 0, 0)),
                  pl.BlockSpec((None, kd, PAGE_SIZE), lambda b, p, pt: (b, 0, 0)),
                  pl.BlockSpec((None, kd, PAGE_SIZE), lambda b, p, pt: (b, 0, 0))]
                 + [page_spec(g) for g in range(n_pg)] * 2,
        out_specs=pl.BlockSpec((None, t_new, N_HEADS * HEAD_W), lambda b, p, pt: (b, 0, 0)),
        scratch_shapes=[pltpu.VMEM((rows, 1), F32),
                        pltpu.VMEM((rows, 1), F32),
                        pltpu.VMEM((rows, HEAD_W), F32)],
    )
    return pl.pallas_call(
        functools.partial(_sample_attn_kernel, t_new=t_new, lam_init=lam_init, n_pg=n_pg),
        grid_spec=grid_spec,
        out_shape=jax.ShapeDtypeStruct((bs, t_new, N_HEADS * HEAD_W), F32),
        compiler_params=_cparams("arbitrary", "arbitrary"),
        name="sample_attn",
    )(page_table, *lam_params, subln_g.reshape(1, HEAD_W), qblk, knew, vnew,
      *([cache_kt] * n_pg), *([cache_v2] * n_pg))


def _block_diag_queries(q, bs, ts):
    ng = N_HEADS // ATT_GHEADS
    nb = 2 * ATT_GHEADS
    x = q.reshape(bs, ts, ng, nb, HEAD_DIM).transpose(0, 2, 3, 1, 4)
    x = jnp.pad(x, ((0, 0), (0, 0), (0, 0), (0, ATT_TPAD - ts), (0, 0)))
    eye = jnp.eye(nb, dtype=q.dtype)[None, None, :, None, :, None]
    x = x[:, :, :, :, None, :] * eye
    return x.reshape(bs, ng, nb * ATT_TPAD, nb * HEAD_DIM)


def _ln_swish(y, lg_ref, lb_ref):
    mu = jnp.mean(y, axis=-1, keepdims=True)
    yc = y - mu
    z = yc * lax.rsqrt(jnp.mean(yc * yc, axis=-1, keepdims=True) + EPS) * lg_ref[...] + lb_ref[...]
    return z * jax.nn.sigmoid(z)


def _conv_prompt_kernel(halo_ref, u_ref, cw_ref, cb_ref, lg_ref, lb_ref, o_ref, ext_s, y_s, *, tt, cc):
    i = pl.program_id(1)
    ext_s[CONV_HALO:, :] = u_ref[...]

    @pl.when(i == 0)
    def _():
        ext_s[0:CONV_HALO, :] = jnp.zeros((CONV_HALO, ext_s.shape[1]), F32)

    @pl.when(i > 0)
    def _():
        ext_s[0:CONV_HALO, :] = halo_ref[...]

    first = CONV_HALO - (CONV_WIDTH - 1)

    def chunk(ci, carry):
        c0 = pl.multiple_of(ci * cc, cc)
        acc = jnp.zeros((tt, cc), F32)
        for r in range(SUBLANES):
            taps = [w for w in range(CONV_WIDTH) if (first + w) % SUBLANES == r]
            rows = tt if r == 0 else tt + SUBLANES
            z = jnp.zeros((rows, cc), F32)
            for w in taps:
                z = z + ext_s[pl.ds(first + w - r, rows), pl.ds(c0, cc)] * cw_ref[w:w + 1, pl.ds(c0, cc)]
            acc = acc + z[r:r + tt]
        y_s[:, pl.ds(c0, cc)] = acc + cb_ref[:, pl.ds(c0, cc)]
        return carry

    lax.fori_loop(0, ext_s.shape[1] // cc, chunk, 0)
    o_ref[...] = _ln_swish(y_s[...], lg_ref, lb_ref).astype(o_ref.dtype)


def _conv_prompt(u3, conv_w, conv_b, ln_g, ln_b, tt):
    b, t, c = u3.shape
    hb = tt // CONV_HALO
    row = pl.BlockSpec((1, c), lambda bi, i: (0, 0))
    return pl.pallas_call(
        functools.partial(_conv_prompt_kernel, tt=tt, cc=256),
        grid=(b, t // tt),
        in_specs=[pl.BlockSpec((None, CONV_HALO, c), lambda bi, i: (bi, jnp.maximum(i * hb - 1, 0), 0)),
                  pl.BlockSpec((None, tt, c), lambda bi, i: (bi, i, 0)),
                  pl.BlockSpec((CONV_WIDTH, c), lambda bi, i: (0, 0)),
                  row, row, row],
        out_specs=pl.BlockSpec((None, tt, c), lambda bi, i: (bi, i, 0)),
        out_shape=jax.ShapeDtypeStruct((b, t, c), BF16),
        scratch_shapes=[pltpu.VMEM((CONV_HALO + tt, c), F32), pltpu.VMEM((tt, c), F32)],
        compiler_params=_cparams("arbitrary", "arbitrary"),
        name="conv_prompt",
    )(u3, u3, conv_w, conv_b.reshape(1, c), ln_g.reshape(1, c), ln_b.reshape(1, c))


def _conv_sample_kernel(ext_ref, cw_ref, cb_ref, lg_ref, lb_ref, o_ref, *, t_new):
    acc = jnp.zeros((t_new, ext_ref.shape[1]), F32)
    for w in range(CONV_WIDTH):
        acc = acc + ext_ref[w:w + t_new, :] * cw_ref[w:w + 1, :]
    o_ref[...] = _ln_swish(acc + cb_ref[...], lg_ref, lb_ref)


def _conv_sample(ext3, conv_w, conv_b, ln_g, ln_b, t_new):
    b, rows, c = ext3.shape
    row = pl.BlockSpec((1, c), lambda bi: (0, 0))
    return pl.pallas_call(
        functools.partial(_conv_sample_kernel, t_new=t_new),
        grid=(b,),
        in_specs=[pl.BlockSpec((None, rows, c), lambda bi: (bi, 0, 0)),
                  pl.BlockSpec((CONV_WIDTH, c), lambda bi: (0, 0)),
                  row, row, row],
        out_specs=pl.BlockSpec((None, t_new, c), lambda bi: (bi, 0, 0)),
        out_shape=jax.ShapeDtypeStruct((b, t_new, c), F32),
        compiler_params=_cparams("arbitrary"),
        name="conv_sample",
    )(ext3, conv_w, conv_b.reshape(1, c), ln_g.reshape(1, c), ln_b.reshape(1, c))


def _out_kernel(o_ref, yc_ref, wt_ref, wb_ref, x_ref, g_ref, out_ref):
    mix = (jnp.dot(o_ref[...], wt_ref[...], preferred_element_type=F32)
           + jnp.dot(yc_ref[...], wb_ref[...], preferred_element_type=F32))
    out_ref[...] = x_ref[...] + g_ref[...] * mix


def _out_proj(o_bf, yc_bf, w_out_bf, x, g3, tm, tn, rows_per_group):
    n, ka = o_bf.shape
    kc = yc_bf.shape[1]
    d = w_out_bf.shape[1]
    assert ka == kc
    return pl.pallas_call(
        _out_kernel,
        grid=(n // tm, d // tn),
        in_specs=[pl.BlockSpec((tm, ka), lambda i, j: (i, 0)),
                  pl.BlockSpec((tm, kc), lambda i, j: (i, 0)),
                  pl.BlockSpec((ka, tn), lambda i, j: (0, j)),
                  pl.BlockSpec((kc, tn), lambda i, j: (1, j)),
                  pl.BlockSpec((tm, tn), lambda i, j: (i, j)),
                  _mod_spec(g3, tm, tn, rows_per_group, True)],
        out_specs=pl.BlockSpec((tm, tn), lambda i, j: (i, j)),
        out_shape=jax.ShapeDtypeStruct((n, d), F32),
        compiler_params=_cparams("arbitrary", "arbitrary"),
        name="out_proj",
    )(o_bf, yc_bf, w_out_bf, w_out_bf, x, g3)


def _drop_first_max(work, rid, fill):
    m = jnp.max(work, axis=0, keepdims=True)
    first = jnp.min(jnp.where(work == m, rid, float(work.shape[0])), axis=0, keepdims=True)
    return m, jnp.where(rid == first, fill, work)


def _row_ids(shape):
    return lax.broadcasted_iota(jnp.int32, shape, 0).astype(F32)


def _top_rows(s, k):
    rid = _row_ids(s.shape)
    rows = []
    work = s
    for _ in range(k):
        m, work = _drop_first_max(work, rid, -jnp.inf)
        rows.append(m)
    return jnp.concatenate(rows, axis=0)


def _kth_largest(c, k):
    rid = _row_ids(c.shape)
    work = c
    for _ in range(k - 1):
        _, work = _drop_first_max(work, rid, -1.0)
    return jnp.max(work, axis=0, keepdims=True)


def _candidates(pa, pb):
    parts = [pa[0:1] * pb]
    for a in range(1, 8):
        parts.append(pa[a:a + 1] * pb[0:8])
    parts.append(pa[8:16] * pb[0:1])
    return jnp.concatenate(parts, axis=0)


def _peer_route_kernel(q_ref, keys_ref, ea_ref, eb_ref, pt_ref):
    q = q_ref[...].astype(BF16)
    half = D_QUERY // 2
    pts = []
    for h in range(PEER_HEADS):
        s1 = _dot_nt(keys_ref[h, 0], q[:, (2 * h) * half:(2 * h + 1) * half])
        s2 = _dot_nt(keys_ref[h, 1], q[:, (2 * h + 1) * half:(2 * h + 2) * half])
        sv1 = _top_rows(s1, PEER_TOPK)
        sv2 = _top_rows(s2, PEER_TOPK)
        m1, m2 = sv1[0:1], sv2[0:1]
        ea = jnp.exp(s1 - m1)
        eb = jnp.exp(s2 - m2)
        pa = jnp.exp(sv1 - m1)
        pb = jnp.exp(sv2 - m2)
        cand = _candidates(pa, pb)
        sel = cand >= _kth_largest(cand, PEER_TOPK)
        zinv = 1.0 / jnp.sum(jnp.where(sel, cand, 0.0), axis=0, keepdims=True)
        cand_n = _candidates(pa, pb * zinv)
        pts.append(jnp.min(jnp.where(sel, cand_n, jnp.inf), axis=0, keepdims=True))
        ea_ref[h] = ea
        eb_ref[h] = eb * zinv
    pt_ref[...] = jnp.concatenate(pts, axis=0)


def _peer_route(q, keys_bf, tt):
    n, dq = q.shape
    return pl.pallas_call(
        _peer_route_kernel,
        grid=(n // tt,),
        in_specs=[pl.BlockSpec((tt, dq), lambda i: (i, 0)),
                  pl.BlockSpec(keys_bf.shape, lambda i: (0, 0, 0, 0))],
        out_specs=[pl.BlockSpec((PEER_HEADS, N_KEYS, tt), lambda i: (0, 0, i)),
                   pl.BlockSpec((PEER_HEADS, N_KEYS, tt), lambda i: (0, 0, i)),
                   pl.BlockSpec((PEER_HEADS, tt), lambda i: (0, i))],
        out_shape=[jax.ShapeDtypeStruct((PEER_HEADS, N_KEYS, n), F32),
                   jax.ShapeDtypeStruct((PEER_HEADS, N_KEYS, n), F32),
                   jax.ShapeDtypeStruct((PEER_HEADS, n), F32)],
        compiler_params=_cparams("arbitrary"),
        name="peer_route",
    )(q, keys_bf)


def _gelu_exact(x):
    return 0.5 * x * (1.0 + lax.erf(x * (2.0 ** -0.5)))


def _peer_dense_kernel(h_ref, u_ref, v_ref, ea_ref, eb_ref, pt_ref, y_ref, *, ec):
    c = pl.program_id(1)

    @pl.when(c == 0)
    def _():
        y_ref[...] = jnp.zeros(y_ref.shape, F32)

    act = _gelu_exact(_dot_nt(u_ref[...], h_ref[...]))
    groups = ec // N_KEYS
    pieces = []
    for ii in range(groups):
        i = c * groups + ii
        gate = jnp.zeros((N_KEYS, act.shape[1]), F32)
        for h in range(PEER_HEADS):
            p = ea_ref[h, pl.ds(i, 1), :] * eb_ref[h]
            gate = gate + jnp.where(p >= pt_ref[h:h + 1, :], p, 0.0)
        pieces.append((gate * act[ii * N_KEYS:(ii + 1) * N_KEYS]).astype(BF16))
    coef = jnp.concatenate(pieces, axis=0)
    y_ref[...] += _dot_tn(coef, v_ref[...])


def _peer_dense(h2, u_bf, v_bf, ea, eb, pt, tt, ec):
    n, d = h2.shape
    ne = u_bf.shape[0]
    return pl.pallas_call(
        functools.partial(_peer_dense_kernel, ec=ec),
        grid=(n // tt, ne // ec),
        in_specs=[pl.BlockSpec((tt, d), lambda i, c: (i, 0)),
                  pl.BlockSpec((ec, d), lambda i, c: (c, 0)),
                  pl.BlockSpec((ec, d), lambda i, c: (c, 0)),
                  pl.BlockSpec((PEER_HEADS, N_KEYS, tt), lambda i, c: (0, 0, i)),
                  pl.BlockSpec((PEER_HEADS, N_KEYS, tt), lambda i, c: (0, 0, i)),
                  pl.BlockSpec((PEER_HEADS, tt), lambda i, c: (0, i))],
        out_specs=pl.BlockSpec((tt, d), lambda i, c: (i, 0)),
        out_shape=jax.ShapeDtypeStruct((n, d), F32),
        compiler_params=_cparams("arbitrary", "arbitrary"),
        name="peer_dense",
    )(h2, u_bf, v_bf, ea, eb, pt)


def _final_kernel(x_ref, y_ref, g2_ref, gf_ref, o_ref):
    x = x_ref[...] + g2_ref[...] * y_ref[...]
    o_ref[...] = _rms(x) * gf_ref[...]


def _final(x1, y, g23, normf_g, tm, rows_per_group):
    n, d = x1.shape
    return pl.pallas_call(
        _final_kernel,
        grid=(n // tm,),
        in_specs=[pl.BlockSpec((tm, d), lambda i: (i, 0)),
                  pl.BlockSpec((tm, d), lambda i: (i, 0)),
                  _mod_spec(g23, tm, d, rows_per_group, False),
                  pl.BlockSpec((1, d), lambda i: (0, 0))],
        out_specs=pl.BlockSpec((tm, d), lambda i: (i, 0)),
        out_shape=jax.ShapeDtypeStruct((n, d), F32),
        compiler_params=_cparams("arbitrary"),
        name="final",
    )(x1, y, g23, normf_g.reshape(1, d))


def _rope_tables(pos):
    inv = ROPE_THETA ** (-jnp.arange(0, HEAD_DIM, 2, dtype=F32) / HEAD_DIM)
    ang = pos.astype(F32)[:, None] * inv[None, :]
    cos, sin = jnp.cos(ang), jnp.sin(ang)
    cos_t = jnp.concatenate([cos, cos, cos, cos], axis=1)
    sin_t = jnp.concatenate([-sin, sin, -sin, sin], axis=1)
    return cos_t, sin_t


def _peer_block(h2, x1, g23, w, tm_rows, rows_per_group, tt_route, tt_dense, ec):
    q = _proj(h2, w["wq"], 0, w["wq"].shape[1], tm_rows, 512, want_f32=True)[0]
    ea, eb, pt = _peer_route(q, w["keys"], tt_route)
    y = _peer_dense(h2, w["u"], w["v"], ea, eb, pt, tt_dense, ec)
    return _final(x1, y, g23, w["normf_g"], min(tm_rows, 256), rows_per_group)


def kernel(x_prompt, x_sample, c_prompt, c_sample, cache_k, cache_v, state_conv, page_table, w_ada, b_ada,
           norm1_g, norm2_g, w_in, lambda_q1, lambda_k1, lambda_q2, lambda_k2, subln_g, conv_w, conv_b,
           conv_ln_g, conv_ln_b, w_out, peer_wq, peer_keys, peer_u, peer_v, normf_g):
    depth = w_ada.shape[0]
    assert depth == 1, "single-layer trunk"
    layer = 0
    lam_init = _lambda_init(layer)
    bp, tp, d = x_prompt.shape
    bs, ts, _ = x_sample.shape
    n_p, n_s = bp * tp, bs * ts
    past = page_table.shape[1] * PAGE_SIZE

    w_in_bf = w_in[layer].astype(BF16)
    w_out_bf = w_out[layer].astype(BF16)
    pw = dict(wq=peer_wq[layer].astype(BF16), keys=peer_keys[layer].astype(BF16),
              u=peer_u[layer].astype(BF16), v=peer_v[layer].astype(BF16), normf_g=normf_g)
    lam_params = [p[layer].reshape(1, HEAD_DIM) for p in (lambda_q1, lambda_k1, lambda_q2, lambda_k2)]

    mod = _ada(jnp.concatenate([c_prompt, c_sample], axis=0), w_ada[layer], b_ada[layer])
    mod_p = [m.reshape(bp, 1, d) for m in jnp.split(mod[:bp], 6, axis=-1)]
    mod_s = [jnp.repeat(m, ts, axis=0).reshape(1, n_s, d) for m in jnp.split(mod[bp:], 6, axis=-1)]

    cos_p, sin_p = _rope_tables(jnp.arange(tp, dtype=jnp.int32))
    pos_s = past + jnp.arange(ts, dtype=jnp.int32)
    cos_s, sin_s = _rope_tables(jnp.tile(pos_s, bs))
    qk_scale = HEAD_DIM ** -0.5

    xp = x_prompt.reshape(n_p, d)
    sh1, sc1, g1, sh2, sc2, g2 = mod_p
    tm = 1024
    h = _norm_mod(xp, norm1_g[layer], sc1, sh1, 256, tp)
    (q_hm,) = _proj(h, w_in_bf, 0, D_ATTN, tm, 512, rope=(cos_p, sin_p), scale=qk_scale, hm_seq=tp)
    k_p, k_hm = _proj(h, w_in_bf, D_ATTN, D_ATTN, tm, 512, rope=(cos_p, sin_p), want_f32=True, hm_seq=tp)
    v_p, v_hm = _proj(h, w_in_bf, 2 * D_ATTN, D_ATTN, tm, 512, want_f32=True, hm_seq=tp)
    u_p = _glu(h, w_in_bf, 3 * D_ATTN, 3 * D_ATTN + D_CONV, D_CONV, tm, 512)
    o_p = _prompt_attn(q_hm, k_hm, v_hm, lam_params, subln_g[layer], lam_init, 512)
    u_p3 = u_p.reshape(bp, tp, D_CONV)
    yc_p = _conv_prompt(u_p3, conv_w[layer], conv_b[layer], conv_ln_g[layer], conv_ln_b[layer], 128)
    x1_p = _out_proj(o_p.reshape(n_p, D_ATTN), yc_p.reshape(n_p, D_CONV), w_out_bf, xp, g1, tm, 512, tp)
    h2_p = _norm_mod(x1_p, norm2_g[layer], sc2, sh2, 256, tp)
    y_prompt = _peer_block(h2_p, x1_p, g2, pw, tm, tp, 256, 512, 512).reshape(bp, tp, d)

    xs = x_sample.reshape(n_s, d)
    sh1, sc1, g1, sh2, sc2, g2 = mod_s
    tm = n_s
    h = _norm_mod(xs, norm1_g[layer], sc1, sh1, tm, n_s)
    (q_s,) = _proj(h, w_in_bf, 0, D_ATTN, tm, 1024, rope=(cos_s, sin_s), scale=qk_scale, want_bf=True)
    (k_s,) = _proj(h, w_in_bf, D_ATTN, D_ATTN, tm, 1024, rope=(cos_s, sin_s), want_f32=True)
    (v_s,) = _proj(h, w_in_bf, 2 * D_ATTN, D_ATTN, tm, 1024, want_f32=True)
    u_s = _glu(h, w_in_bf, 3 * D_ATTN, 3 * D_ATTN + D_CONV, D_CONV, tm, 1024)
    qblk = _block_diag_queries(q_s, bs, ts)
    knew = jnp.pad(k_s.reshape(bs, ts, D_ATTN).transpose(0, 2, 1), ((0, 0), (0, 0), (0, PAGE_SIZE - ts)))
    vnew = jnp.pad(v_s.reshape(bs, ts * N_HEADS, HEAD_W), ((0, 0), (0, (PAGE_SIZE - ts) * N_HEADS), (0, 0)))
    n_pool = cache_k.shape[1]
    cache_kt = cache_k[layer].transpose(0, 2, 3, 4, 1).reshape(n_pool, D_ATTN, PAGE_SIZE)
    cache_v2 = cache_v[layer].reshape(n_pool, PAGE_SIZE * N_HEADS, HEAD_W)
    o_s = _sample_attn(page_table, qblk, knew, vnew, cache_kt, cache_v2,
                       lam_params, subln_g[layer], lam_init, ts, 8)
    ext_s = jnp.concatenate([state_conv[layer], u_s.reshape(bs, ts, D_CONV)], axis=1)
    yc_s = _conv_sample(ext_s, conv_w[layer], conv_b[layer], conv_ln_g[layer], conv_ln_b[layer], ts)
    x1_s = _out_proj(o_s.reshape(n_s, D_ATTN).astype(BF16), yc_s.reshape(n_s, D_CONV).astype(BF16),
                     w_out_bf, xs, g1, tm, 1024, n_s)
    h2_s = _norm_mod(x1_s, norm2_g[layer], sc2, sh2, tm, n_s)
    y_sample = _peer_block(h2_s, x1_s, g2, pw, tm, n_s, n_s, n_s, 512).reshape(bs, ts, d)

    new_k_p = k_p.reshape(1, bp, tp, N_HEADS, 2, HEAD_DIM)
    new_v_p = v_p.reshape(1, bp, tp, N_HEADS, HEAD_W)
    new_conv_p = u_p3[:, tp - (CONV_WIDTH - 1):][None]
    new_k_s = k_s.reshape(1, bs, ts, N_HEADS, 2, HEAD_DIM)
    new_v_s = v_s.reshape(1, bs, ts, N_HEADS, HEAD_W)
    new_conv_s = ext_s[:, ts:][None]
    return (y_prompt, y_sample, new_k_p, new_v_p, new_conv_p, new_k_s, new_v_s, new_conv_s)
```

```python
import functools
import math

import jax
import jax.numpy as jnp
from jax import lax
from jax.experimental import pallas as pl
from jax.experimental.pallas import tpu as pltpu

F32 = jnp.float32
BF16 = jnp.bfloat16

D_MODEL = 4096
D_ATTN = 2048
D_CONV = 2048
HEAD_DIM = 64
HEAD_W = 2 * HEAD_DIM
N_HEADS = D_ATTN // HEAD_W
CONV_WIDTH = 31
CONV_HALO = 32
ROPE_THETA = 10000.0
PAGE_SIZE = 128
PEER_HEADS = 8
N_KEYS = 128
N_EXPERTS = N_KEYS * N_KEYS
PEER_TOPK = 16
D_QUERY = 256
EPS = 1e-6
NEG_INF = -1e30
LANES = 128
SUBLANES = 8
VMEM_LIMIT = 56 * 1024 * 1024


def _lambda_init(layer_idx):
    return 0.8 - 0.6 * math.exp(-0.3 * layer_idx)


def _cparams(*sem):
    return pltpu.CompilerParams(dimension_semantics=sem, vmem_limit_bytes=VMEM_LIMIT)


def _dot_nt(a, b):
    return lax.dot_general(a, b, (((1,), (1,)), ((), ())), preferred_element_type=F32)


def _dot_tn(a, b):
    return lax.dot_general(a, b, (((0,), (0,)), ((), ())), preferred_element_type=F32)


def _ada_kernel(c_ref, w_ref, b_ref, o_ref):
    c = c_ref[...]
    a = (c * jax.nn.sigmoid(c)).astype(BF16)
    o_ref[...] = jnp.dot(a, w_ref[...].astype(BF16), preferred_element_type=F32) + b_ref[...]


def _ada(c_all, w_ada, b_ada):
    m, d = c_all.shape
    n = w_ada.shape[1]
    tn = 1024
    return pl.pallas_call(
        _ada_kernel,
        grid=(n // tn,),
        in_specs=[pl.BlockSpec((m, d), lambda j: (0, 0)),
                  pl.BlockSpec((d, tn), lambda j: (0, j)),
                  pl.BlockSpec((1, tn), lambda j: (0, j))],
        out_specs=pl.BlockSpec((m, tn), lambda j: (0, j)),
        out_shape=jax.ShapeDtypeStruct((m, n), F32),
        compiler_params=_cparams("arbitrary"),
        name="ada",
    )(c_all, w_ada, b_ada.reshape(1, n))


def _mod_spec(mod3, tm, tn, rows_per_group, with_j):
    per_token = mod3.shape[1] != 1
    if per_token:
        if with_j:
            return pl.BlockSpec((None, tm, tn), lambda i, j: (0, i, j))
        return pl.BlockSpec((None, tm, tn), lambda i: (0, i, 0))
    tiles_per_group = rows_per_group // tm
    if with_j:
        return pl.BlockSpec((None, 1, tn), lambda i, j: (i // tiles_per_group, 0, j))
    return pl.BlockSpec((None, 1, tn), lambda i: (i // tiles_per_group, 0, 0))


def _rms(x):
    return x * lax.rsqrt(jnp.mean(x * x, axis=-1, keepdims=True) + EPS)


def _norm_mod_kernel(x_ref, g_ref, sc_ref, sh_ref, o_ref):
    y = _rms(x_ref[...]) * g_ref[...]
    o_ref[...] = (y * (1.0 + sc_ref[...]) + sh_ref[...]).astype(o_ref.dtype)


def _norm_mod(x, g, sc3, sh3, tm, rows_per_group):
    n, d = x.shape
    return pl.pallas_call(
        _norm_mod_kernel,
        grid=(n // tm,),
        in_specs=[pl.BlockSpec((tm, d), lambda i: (i, 0)),
                  pl.BlockSpec((1, d), lambda i: (0, 0)),
                  _mod_spec(sc3, tm, d, rows_per_group, False),
                  _mod_spec(sh3, tm, d, rows_per_group, False)],
        out_specs=pl.BlockSpec((tm, d), lambda i: (i, 0)),
        out_shape=jax.ShapeDtypeStruct((n, d), BF16),
        compiler_params=_cparams("arbitrary"),
        name="norm_mod",
    )(x, g.reshape(1, d), sc3, sh3)


def _rope_apply(z, cos_ref, sin_ref):
    cos, sin = cos_ref[...], sin_ref[...]
    lane = lax.broadcasted_iota(jnp.int32, cos.shape, 1)
    first = (lane & (HEAD_DIM - 1)) < HEAD_DIM // 2
    half = HEAD_DIM // 2
    outs = []
    for g in range(z.shape[1] // LANES):
        zg = z[:, g * LANES:(g + 1) * LANES]
        partner = jnp.where(first, pltpu.roll(zg, LANES - half, 1), pltpu.roll(zg, half, 1))
        outs.append(zg * cos + partner * sin)
    return jnp.concatenate(outs, axis=1)


def _proj_kernel(*refs, rope, scale, want_f32, want_t, want_bf, want_hm):
    h_ref, w_ref = refs[0], refs[1]
    pos = 2
    z = jnp.dot(h_ref[...], w_ref[...], preferred_element_type=F32)
    if rope:
        z = _rope_apply(z, refs[2], refs[3])
        pos = 4
    outs = refs[pos:]
    k = 0
    if want_f32:
        outs[k][...] = z
        k += 1
    if want_t:
        outs[k][...] = z.T
        k += 1
    zs = z * scale if scale != 1.0 else z
    if want_bf:
        outs[k][...] = zs.astype(BF16)
        k += 1
    if want_hm:
        for hh in range(z.shape[1] // HEAD_W):
            outs[k][hh] = zs[:, hh * HEAD_W:(hh + 1) * HEAD_W].astype(BF16)


def _proj(h, w_bf, col0, ncols, tm, tn, *, rope=None, scale=1.0, want_f32=False, want_bf=False,
          hm_seq=None, t_seq=None):
    n, kdim = h.shape
    joff = col0 // tn
    in_specs = [pl.BlockSpec((tm, kdim), lambda i, j: (i, 0)),
                pl.BlockSpec((kdim, tn), lambda i, j: (0, j + joff))]
    args = [h, w_bf]
    if rope is not None:
        cos, sin = rope
        tiles = cos.shape[0] // tm
        in_specs += [pl.BlockSpec((tm, LANES), lambda i, j: (i % tiles, 0))] * 2
        args += [cos, sin]
    out_specs, out_shape = [], []
    if want_f32:
        out_specs.append(pl.BlockSpec((tm, tn), lambda i, j: (i, j)))
        out_shape.append(jax.ShapeDtypeStruct((n, ncols), F32))
    if t_seq is not None:
        tpt = t_seq // tm
        out_specs.append(pl.BlockSpec((None, tn, tm), lambda i, j: (i // tpt, j, i % tpt)))
        out_shape.append(jax.ShapeDtypeStruct((n // t_seq, ncols, t_seq), F32))
    if want_bf:
        out_specs.append(pl.BlockSpec((tm, tn), lambda i, j: (i, j)))
        out_shape.append(jax.ShapeDtypeStruct((n, ncols), BF16))
    if hm_seq is not None:
        tps = hm_seq // tm
        hpt = tn // HEAD_W
        out_specs.append(pl.BlockSpec((None, hpt, tm, HEAD_W), lambda i, j: (i // tps, j, i % tps, 0)))
        out_shape.append(jax.ShapeDtypeStruct((n // hm_seq, ncols // HEAD_W, hm_seq, HEAD_W), BF16))
    kern = functools.partial(_proj_kernel, rope=rope is not None, scale=scale, want_f32=want_f32,
                             want_t=t_seq is not None, want_bf=want_bf, want_hm=hm_seq is not None)
    return pl.pallas_call(
        kern,
        grid=(n // tm, ncols // tn),
        in_specs=in_specs,
        out_specs=out_specs,
        out_shape=out_shape,
        compiler_params=_cparams("arbitrary", "arbitrary"),
        name="proj",
    )(*args)


def _glu_kernel(h_ref, wa_ref, wg_ref, o_ref):
    h = h_ref[...]
    a = jnp.dot(h, wa_ref[...], preferred_element_type=F32)
    g = jnp.dot(h, wg_ref[...], preferred_element_type=F32)
    o_ref[...] = a * jax.nn.sigmoid(g)


def _glu(h, w_bf, col_a, col_g, ncols, tm, tn):
    n, kdim = h.shape
    ja, jg = col_a // tn, col_g // tn
    return pl.pallas_call(
        _glu_kernel,
        grid=(n // tm, ncols // tn),
        in_specs=[pl.BlockSpec((tm, kdim), lambda i, j: (i, 0)),
                  pl.BlockSpec((kdim, tn), lambda i, j: (0, j + ja)),
                  pl.BlockSpec((kdim, tn), lambda i, j: (0, j + jg))],
        out_specs=pl.BlockSpec((tm, tn), lambda i, j: (i, j)),
        out_shape=jax.ShapeDtypeStruct((n, ncols), F32),
        compiler_params=_cparams("arbitrary", "arbitrary"),
        name="glu",
    )(h, w_bf, w_bf)


def _lam_value(lq1, lk1, lq2, lk2, lam_init):
    a = jnp.exp(jnp.sum(lq1[...] * lk1[...], axis=-1, keepdims=True))
    b = jnp.exp(jnp.sum(lq2[...] * lk2[...], axis=-1, keepdims=True))
    return a - b + lam_init


def _prompt_attn_kernel(lq1, lk1, lq2, lk2, g_ref, q_ref, k_ref, v_ref, o_ref, qs_s, m_s, l_s, acc_s, *,
                        tq, lam_init):
    qi = pl.program_id(2)
    q = q_ref[...]
    lane = lax.broadcasted_iota(jnp.int32, q.shape, 1)
    zero = jnp.zeros_like(q)
    qs_s[0:tq, :] = jnp.where(lane < HEAD_DIM, q, zero)
    qs_s[tq:2 * tq, :] = jnp.where(lane >= HEAD_DIM, q, zero)
    m_s[...] = jnp.full(m_s.shape, NEG_INF, F32)
    l_s[...] = jnp.zeros(l_s.shape, F32)
    acc_s[...] = jnp.zeros(acc_s.shape, F32)

    def step(j, masked):
        start = pl.multiple_of(j * tq, tq)
        kj = k_ref[pl.ds(start, tq), :]
        vj = v_ref[pl.ds(start, tq), :]
        s = _dot_nt(qs_s[...], kj)
        if masked:
            row = lax.broadcasted_iota(jnp.int32, s.shape, 0)
            row = jnp.where(row >= tq, row - tq, row)
            col = lax.broadcasted_iota(jnp.int32, s.shape, 1)
            s = jnp.where(col <= row, s, NEG_INF)
        m_old = m_s[...]
        m_new = jnp.maximum(m_old, jnp.max(s, axis=-1, keepdims=True))
        alpha = jnp.exp(m_old - m_new)
        p = jnp.exp(s - jnp.concatenate([m_new] * (tq // LANES), axis=1))
        l_s[...] = alpha * l_s[...] + jnp.sum(p, axis=-1, keepdims=True)
        m_s[...] = m_new
        acc_s[...] = alpha * acc_s[...] + jnp.dot(p.astype(BF16), vj, preferred_element_type=F32)

    def body(j, carry):
        step(j, False)
        return carry

    lax.fori_loop(0, qi, body, 0)
    step(qi, True)
    lam = _lam_value(lq1, lk1, lq2, lk2, lam_init)
    o = acc_s[0:tq] / l_s[0:tq] - lam * (acc_s[tq:2 * tq] / l_s[tq:2 * tq])
    o_ref[...] = (_rms(o) * g_ref[...] * (1.0 - lam_init)).astype(o_ref.dtype)


def _prompt_attn(q_hm, k_hm, v_hm, lam_params, subln_g, lam_init, tq):
    b, h, t, _ = q_hm.shape
    vec = pl.BlockSpec((1, HEAD_DIM), lambda bi, hi, qi: (0, 0))
    return pl.pallas_call(
        functools.partial(_prompt_attn_kernel, tq=tq, lam_init=lam_init),
        grid=(b, h, t // tq),
        in_specs=[vec, vec, vec, vec,
                  pl.BlockSpec((1, HEAD_W), lambda bi, hi, qi: (0, 0)),
                  pl.BlockSpec((None, None, tq, HEAD_W), lambda bi, hi, qi: (bi, hi, qi, 0)),
                  pl.BlockSpec((None, None, t, HEAD_W), lambda bi, hi, qi: (bi, hi, 0, 0)),
                  pl.BlockSpec((None, None, t, HEAD_W), lambda bi, hi, qi: (bi, hi, 0, 0))],
        out_specs=pl.BlockSpec((None, tq, HEAD_W), lambda bi, hi, qi: (bi, qi, hi)),
        out_shape=jax.ShapeDtypeStruct((b, t, h * HEAD_W), BF16),
        scratch_shapes=[pltpu.VMEM((2 * tq, HEAD_W), BF16), pltpu.VMEM((2 * tq, LANES), F32),
                        pltpu.VMEM((2 * tq, LANES), F32), pltpu.VMEM((2 * tq, HEAD_W), F32)],
        compiler_params=_cparams("arbitrary", "arbitrary", "arbitrary"),
        name="prompt_attn",
    )(*lam_params, subln_g.reshape(1, HEAD_W), q_hm, k_hm, v_hm)


ATT_TPAD = 8
ATT_HROWS = 2 * ATT_TPAD
ATT_GHEADS = 2


def _sample_attn_kernel(*refs, t_new, lam_init, n_pg):
    lq1, lk1, lq2, lk2, g_ref, qblk_ref, knew_ref, vnew_ref = refs[1:9]
    k_refs = refs[9:9 + n_pg]
    v_refs = refs[9 + n_pg:9 + 2 * n_pg]
    o_ref, m_s, l_s, acc_s = refs[9 + 2 * n_pg:]
    p = pl.program_id(1)
    gk = ATT_GHEADS * HEAD_W
    n_groups = N_HEADS // ATT_GHEADS

    def update(k_ref, v_ref, keep):
        s = jnp.concatenate(
            [jnp.dot(qblk_ref[g], k_ref[g * gk:(g + 1) * gk, :].astype(BF16), preferred_element_type=F32)
             for g in range(n_groups)], axis=0)
        if keep is not None:
            s = jnp.where(keep, s, NEG_INF)
        m_old = m_s[...]
        m_new = jnp.maximum(m_old, jnp.max(s, axis=-1, keepdims=True))
        alpha = jnp.exp(m_old - m_new)
        pr = jnp.exp(s - m_new)
        l_s[...] = alpha * l_s[...] + jnp.sum(pr, axis=-1, keepdims=True)
        m_s[...] = m_new
        pv = jnp.concatenate(
            [jnp.dot(pr[h * ATT_HROWS:(h + 1) * ATT_HROWS].astype(BF16),
                     v_ref[pl.ds(h, PAGE_SIZE, stride=N_HEADS), :].astype(BF16),
                     preferred_element_type=F32)
             for h in range(N_HEADS)], axis=0)
        acc_s[...] = alpha * acc_s[...] + pv

    @pl.when(p == 0)
    def _():
        m_s[...] = jnp.full(m_s.shape, NEG_INF, F32)
        l_s[...] = jnp.zeros(l_s.shape, F32)
        acc_s[...] = jnp.zeros(acc_s.shape, F32)
        r = lax.broadcasted_iota(jnp.int32, (acc_s.shape[0], PAGE_SIZE), 0)
        c = lax.broadcasted_iota(jnp.int32, (acc_s.shape[0], PAGE_SIZE), 1)
        keep = c <= jnp.minimum(r & (ATT_TPAD - 1), t_new - 1)
        update(knew_ref, vnew_ref, keep)

    for g in range(n_pg):
        update(k_refs[g], v_refs[g], None)

    @pl.when(p == pl.num_programs(1) - 1)
    def _():
        lam = _lam_value(lq1, lk1, lq2, lk2, lam_init)
        gain = g_ref[...]
        for h in range(N_HEADS):
            blk = acc_s[h * ATT_HROWS:(h + 1) * ATT_HROWS, :] / l_s[h * ATT_HROWS:(h + 1) * ATT_HROWS, :]
            o = blk[0:t_new] - lam * blk[ATT_TPAD:ATT_TPAD + t_new]
            o_ref[:, h * HEAD_W:(h + 1) * HEAD_W] = _rms(o) * gain * (1.0 - lam_init)


def _sample_attn(page_table, qblk, knew, vnew, cache_kt, cache_v2, lam_params, subln_g, lam_init, t_new, n_pg):
    bs, n_pages = page_table.shape
    rows = N_HEADS * ATT_HROWS
    kd = cache_kt.shape[1]
    vec = pl.BlockSpec((1, HEAD_DIM), lambda b, p, pt: (0, 0))

    def page_spec(g):
        return pl.BlockSpec((None, kd, PAGE_SIZE), lambda b, p, pt: (pt[b, p * n_pg + g], 0, 0))

    grid_spec = pltpu.PrefetchScalarGridSpec(
        num_scalar_prefetch=1,
        grid=(bs, n_pages // n_pg),
        in_specs=[vec, vec, vec, vec,
                  pl.BlockSpec((1, HEAD_W), lambda b, p, pt: (0, 0)),
                  pl.BlockSpec((None,) + qblk.shape[1:], lambda b, p, pt: (b, 0, 0, 0)),
                  pl.BlockSpec((None, kd, PAGE_SIZE), lambda b, p, pt: (b, 0, 0)),
                  pl.BlockSpec((None, kd, PAGE_SIZE), lambda b, p, pt: (b, 0, 0))]
                 + [page_spec(g) for g in range(n_pg)] * 2,
        out_specs=pl.BlockSpec((None, t_new, N_HEADS * HEAD_W), lambda b, p, pt: (b, 0, 0)),
        scratch_shapes=[pltpu.VMEM((rows, 1), F32),
                        pltpu.VMEM((rows, 1), F32),
                        pltpu.VMEM((rows, HEAD_W), F32)],
    )
    return pl.pallas_call(
        functools.partial(_sample_attn_kernel, t_new=t_new, lam_init=lam_init, n_pg=n_pg),
        grid_spec=grid_spec,
        out_shape=jax.ShapeDtypeStruct((bs, t_new, N_HEADS * HEAD_W), F32),
        compiler_params=_cparams("arbitrary", "arbitrary"),
        name="sample_attn",
    )(page_table, *lam_params, subln_g.reshape(1, HEAD_W), qblk, knew, vnew,
      *([cache_kt] * n_pg), *([cache_v2] * n_pg))


def _block_diag_queries(q, bs, ts):
    ng = N_HEADS // ATT_GHEADS
    nb = 2 * ATT_GHEADS
    x = q.reshape(bs, ts, ng, nb, HEAD_DIM).transpose(0, 2, 3, 1, 4)
    x = jnp.pad(x, ((0, 0), (0, 0), (0, 0), (0, ATT_TPAD - ts), (0, 0)))
    eye = jnp.eye(nb, dtype=q.dtype)[None, None, :, None, :, None]
    x = x[:, :, :, :, None, :] * eye
    return x.reshape(bs, ng, nb * ATT_TPAD, nb * HEAD_DIM)


def _ln_swish(y, lg_ref, lb_ref):
    mu = jnp.mean(y, axis=-1, keepdims=True)
    yc = y - mu
    z = yc * lax.rsqrt(jnp.mean(yc * yc, axis=-1, keepdims=True) + EPS) * lg_ref[...] + lb_ref[...]
    return z * jax.nn.sigmoid(z)


def _conv_prompt_kernel(halo_ref, u_ref, cw_ref, cb_ref, lg_ref, lb_ref, o_ref, ext_s, y_s, *, tt, cc):
    i = pl.program_id(1)
    ext_s[CONV_HALO:, :] = u_ref[...]

    @pl.when(i == 0)
    def _():
        ext_s[0:CONV_HALO, :] = jnp.zeros((CONV_HALO, ext_s.shape[1]), F32)

    @pl.when(i > 0)
    def _():
        ext_s[0:CONV_HALO, :] = halo_ref[...]

    first = CONV_HALO - (CONV_WIDTH - 1)

    def chunk(ci, carry):
        c0 = pl.multiple_of(ci * cc, cc)
        acc = jnp.zeros((tt, cc), F32)
        for r in range(SUBLANES):
            taps = [w for w in range(CONV_WIDTH) if (first + w) % SUBLANES == r]
            rows = tt if r == 0 else tt + SUBLANES
            z = jnp.zeros((rows, cc), F32)
            for w in taps:
                z = z + ext_s[pl.ds(first + w - r, rows), pl.ds(c0, cc)] * cw_ref[w:w + 1, pl.ds(c0, cc)]
            acc = acc + z[r:r + tt]
        y_s[:, pl.ds(c0, cc)] = acc + cb_ref[:, pl.ds(c0, cc)]
        return carry

    lax.fori_loop(0, ext_s.shape[1] // cc, chunk, 0)
    o_ref[...] = _ln_swish(y_s[...], lg_ref, lb_ref).astype(o_ref.dtype)


def _conv_prompt(u3, conv_w, conv_b, ln_g, ln_b, tt):
    b, t, c = u3.shape
    hb = tt // CONV_HALO
    row = pl.BlockSpec((1, c), lambda bi, i: (0, 0))
    return pl.pallas_call(
        functools.partial(_conv_prompt_kernel, tt=tt, cc=256),
        grid=(b, t // tt),
        in_specs=[pl.BlockSpec((None, CONV_HALO, c), lambda bi, i: (bi, jnp.maximum(i * hb - 1, 0), 0)),
                  pl.BlockSpec((None, tt, c), lambda bi, i: (bi, i, 0)),
                  pl.BlockSpec((CONV_WIDTH, c), lambda bi, i: (0, 0)),
                  row, row, row],
        out_specs=pl.BlockSpec((None, tt, c), lambda bi, i: (bi, i, 0)),
        out_shape=jax.ShapeDtypeStruct((b, t, c), BF16),
        scratch_shapes=[pltpu.VMEM((CONV_HALO + tt, c), F32), pltpu.VMEM((tt, c), F32)],
        compiler_params=_cparams("arbitrary", "arbitrary"),
        name="conv_prompt",
    )(u3, u3, conv_w, conv_b.reshape(1, c), ln_g.reshape(1, c), ln_b.reshape(1, c))


def _conv_sample_kernel(ext_ref, cw_ref, cb_ref, lg_ref, lb_ref, o_ref, *, t_new):
    acc = jnp.zeros((t_new, ext_ref.shape[1]), F32)
    for w in range(CONV_WIDTH):
        acc = acc + ext_ref[w:w + t_new, :] * cw_ref[w:w + 1, :]
    o_ref[...] = _ln_swish(acc + cb_ref[...], lg_ref, lb_ref)


def _conv_sample(ext3, conv_w, conv_b, ln_g, ln_b, t_new):
    b, rows, c = ext3.shape
    row = pl.BlockSpec((1, c), lambda bi: (0, 0))
    return pl.pallas_call(
        functools.partial(_conv_sample_kernel, t_new=t_new),
        grid=(b,),
        in_specs=[pl.BlockSpec((None, rows, c), lambda bi: (bi, 0, 0)),
                  pl.BlockSpec((CONV_WIDTH, c), lambda bi: (0, 0)),
                  row, row, row],
        out_specs=pl.BlockSpec((None, t_new, c), lambda bi: (bi, 0, 0)),
        out_shape=jax.ShapeDtypeStruct((b, t_new, c), F32),
        compiler_params=_cparams("arbitrary"),
        name="conv_sample",
    )(ext3, conv_w, conv_b.reshape(1, c), ln_g.reshape(1, c), ln_b.reshape(1, c))


def _out_kernel(o_ref, yc_ref, wt_ref, wb_ref, x_ref, g_ref, out_ref):
    mix = (jnp.dot(o_ref[...], wt_ref[...], preferred_element_type=F32)
           + jnp.dot(yc_ref[...], wb_ref[...], preferred_element_type=F32))
    out_ref[...] = x_ref[...] + g_ref[...] * mix


def _out_proj(o_bf, yc_bf, w_out_bf, x, g3, tm, tn, rows_per_group):
    n, ka = o_bf.shape
    kc = yc_bf.shape[1]
    d = w_out_bf.shape[1]
    assert ka == kc
    return pl.pallas_call(
        _out_kernel,
        grid=(n // tm, d // tn),
        in_specs=[pl.BlockSpec((tm, ka), lambda i, j: (i, 0)),
                  pl.BlockSpec((tm, kc), lambda i, j: (i, 0)),
                  pl.BlockSpec((ka, tn), lambda i, j: (0, j)),
                  pl.BlockSpec((kc, tn), lambda i, j: (1, j)),
                  pl.BlockSpec((tm, tn), lambda i, j: (i, j)),
                  _mod_spec(g3, tm, tn, rows_per_group, True)],
        out_specs=pl.BlockSpec((tm, tn), lambda i, j: (i, j)),
        out_shape=jax.ShapeDtypeStruct((n, d), F32),
        compiler_params=_cparams("arbitrary", "arbitrary"),
        name="out_proj",
    )(o_bf, yc_bf, w_out_bf, w_out_bf, x, g3)


def _drop_first_max(work, rid, fill):
    m = jnp.max(work, axis=0, keepdims=True)
    first = jnp.min(jnp.where(work == m, rid, float(work.shape[0])), axis=0, keepdims=True)
    return m, jnp.where(rid == first, fill, work)


def _row_ids(shape):
    return lax.broadcasted_iota(jnp.int32, shape, 0).astype(F32)


def _top_rows(s, k):
    rid = _row_ids(s.shape)
    rows = []
    work = s
    for _ in range(k):
        m, work = _drop_first_max(work, rid, -jnp.inf)
        rows.append(m)
    return jnp.concatenate(rows, axis=0)


def _kth_largest(c, k):
    rid = _row_ids(c.shape)
    work = c
    for _ in range(k - 1):
        _, work = _drop_first_max(work, rid, -1.0)
    return jnp.max(work, axis=0, keepdims=True)


def _candidates(pa, pb):
    parts = [pa[0:1] * pb]
    for a in range(1, 8):
        parts.append(pa[a:a + 1] * pb[0:8])
    parts.append(pa[8:16] * pb[0:1])
    return jnp.concatenate(parts, axis=0)


def _peer_route_kernel(q_ref, keys_ref, ea_ref, eb_ref, pt_ref):
    q = q_ref[...].astype(BF16)
    half = D_QUERY // 2
    pts = []
    for h in range(PEER_HEADS):
        s1 = _dot_nt(keys_ref[h, 0], q[:, (2 * h) * half:(2 * h + 1) * half])
        s2 = _dot_nt(keys_ref[h, 1], q[:, (2 * h + 1) * half:(2 * h + 2) * half])
        sv1 = _top_rows(s1, PEER_TOPK)
        sv2 = _top_rows(s2, PEER_TOPK)
        m1, m2 = sv1[0:1], sv2[0:1]
        ea = jnp.exp(s1 - m1)
        eb = jnp.exp(s2 - m2)
        pa = jnp.exp(sv1 - m1)
        pb = jnp.exp(sv2 - m2)
        cand = _candidates(pa, pb)
        sel = cand >= _kth_largest(cand, PEER_TOPK)
        zinv = 1.0 / jnp.sum(jnp.where(sel, cand, 0.0), axis=0, keepdims=True)
        cand_n = _candidates(pa, pb * zinv)
        pts.append(jnp.min(jnp.where(sel, cand_n, jnp.inf), axis=0, keepdims=True))
        ea_ref[h] = ea
        eb_ref[h] = eb * zinv
    pt_ref[...] = jnp.concatenate(pts, axis=0)


def _peer_route(q, keys_bf, tt):
    n, dq = q.shape
    return pl.pallas_call(
        _peer_route_kernel,
        grid=(n // tt,),
        in_specs=[pl.BlockSpec((tt, dq), lambda i: (i, 0)),
                  pl.BlockSpec(keys_bf.shape, lambda i: (0, 0, 0, 0))],
        out_specs=[pl.BlockSpec((PEER_HEADS, N_KEYS, tt), lambda i: (0, 0, i)),
                   pl.BlockSpec((PEER_HEADS, N_KEYS, tt), lambda i: (0, 0, i)),
                   pl.BlockSpec((PEER_HEADS, tt), lambda i: (0, i))],
        out_shape=[jax.ShapeDtypeStruct((PEER_HEADS, N_KEYS, n), F32),
                   jax.ShapeDtypeStruct((PEER_HEADS, N_KEYS, n), F32),
                   jax.ShapeDtypeStruct((PEER_HEADS, n), F32)],
        compiler_params=_cparams("arbitrary"),
        name="peer_route",
    )(q, keys_bf)


def _gelu_exact(x):
    return 0.5 * x * (1.0 + lax.erf(x * (2.0 ** -0.5)))


def _peer_dense_kernel(h_ref, u_ref, v_ref, ea_ref, eb_ref, pt_ref, y_ref, *, ec):
    c = pl.program_id(1)

    @pl.when(c == 0)
    def _():
        y_ref[...] = jnp.zeros(y_ref.shape, F32)

    act = _gelu_exact(_dot_nt(u_ref[...], h_ref[...]))
    groups = ec // N_KEYS
    pieces = []
    for ii in range(groups):
        i = c * groups + ii
        gate = jnp.zeros((N_KEYS, act.shape[1]), F32)
        for h in range(PEER_HEADS):
            p = ea_ref[h, pl.ds(i, 1), :] * eb_ref[h]
            gate = gate + jnp.where(p >= pt_ref[h:h + 1, :], p, 0.0)
        pieces.append((gate * act[ii * N_KEYS:(ii + 1) * N_KEYS]).astype(BF16))
    coef = jnp.concatenate(pieces, axis=0)
    y_ref[...] += _dot_tn(coef, v_ref[...])


def _peer_dense(h2, u_bf, v_bf, ea, eb, pt, tt, ec):
    n, d = h2.shape
    ne = u_bf.shape[0]
    return pl.pallas_call(
        functools.partial(_peer_dense_kernel, ec=ec),
        grid=(n // tt, ne // ec),
        in_specs=[pl.BlockSpec((tt, d), lambda i, c: (i, 0)),
                  pl.BlockSpec((ec, d), lambda i, c: (c, 0)),
                  pl.BlockSpec((ec, d), lambda i, c: (c, 0)),
                  pl.BlockSpec((PEER_HEADS, N_KEYS, tt), lambda i, c: (0, 0, i)),
                  pl.BlockSpec((PEER_HEADS, N_KEYS, tt), lambda i, c: (0, 0, i)),
                  pl.BlockSpec((PEER_HEADS, tt), lambda i, c: (0, i))],
        out_specs=pl.BlockSpec((tt, d), lambda i, c: (i, 0)),
        out_shape=jax.ShapeDtypeStruct((n, d), F32),
        compiler_params=_cparams("arbitrary", "arbitrary"),
        name="peer_dense",
    )(h2, u_bf, v_bf, ea, eb, pt)


def _final_kernel(x_ref, y_ref, g2_ref, gf_ref, o_ref):
    x = x_ref[...] + g2_ref[...] * y_ref[...]
    o_ref[...] = _rms(x) * gf_ref[...]


def _final(x1, y, g23, normf_g, tm, rows_per_group):
    n, d = x1.shape
    return pl.pallas_call(
        _final_kernel,
        grid=(n // tm,),
        in_specs=[pl.BlockSpec((tm, d), lambda i: (i, 0)),
                  pl.BlockSpec((tm, d), lambda i: (i, 0)),
                  _mod_spec(g23, tm, d, rows_per_group, False),
                  pl.BlockSpec((1, d), lambda i: (0, 0))],
        out_specs=pl.BlockSpec((tm, d), lambda i: (i, 0)),
        out_shape=jax.ShapeDtypeStruct((n, d), F32),
        compiler_params=_cparams("arbitrary"),
        name="final",
    )(x1, y, g23, normf_g.reshape(1, d))


def _rope_tables(pos):
    inv = ROPE_THETA ** (-jnp.arange(0, HEAD_DIM, 2, dtype=F32) / HEAD_DIM)
    ang = pos.astype(F32)[:, None] * inv[None, :]
    cos, sin = jnp.cos(ang), jnp.sin(ang)
    cos_t = jnp.concatenate([cos, cos, cos, cos], axis=1)
    sin_t = jnp.concatenate([-sin, sin, -sin, sin], axis=1)
    return cos_t, sin_t


def _peer_block(h2, x1, g23, w, tm_rows, rows_per_group, tt_route, tt_dense, ec):
    q = _proj(h2, w["wq"], 0, w["wq"].shape[1], tm_rows, 512, want_f32=True)[0]
    ea, eb, pt = _peer_route(q, w["keys"], tt_route)
    y = _peer_dense(h2, w["u"], w["v"], ea, eb, pt, tt_dense, ec)
    return _final(x1, y, g23, w["normf_g"], min(tm_rows, 256), rows_per_group)


def kernel(x_prompt, x_sample, c_prompt, c_sample, cache_k, cache_v, state_conv, page_table, w_ada, b_ada,
           norm1_g, norm2_g, w_in, lambda_q1, lambda_k1, lambda_q2, lambda_k2, subln_g, conv_w, conv_b,
           conv_ln_g, conv_ln_b, w_out, peer_wq, peer_keys, peer_u, peer_v, normf_g):
    depth = w_ada.shape[0]
    assert depth == 1, "single-layer trunk"
    layer = 0
    lam_init = _lambda_init(layer)
    bp, tp, d = x_prompt.shape
    bs, ts, _ = x_sample.shape
    n_p, n_s = bp * tp, bs * ts
    past = page_table.shape[1] * PAGE_SIZE

    w_in_bf = w_in[layer].astype(BF16)
    w_out_bf = w_out[layer].astype(BF16)
    pw = dict(wq=peer_wq[layer].astype(BF16), keys=peer_keys[layer].astype(BF16),
              u=peer_u[layer].astype(BF16), v=peer_v[layer].astype(BF16), normf_g=normf_g)
    lam_params = [p[layer].reshape(1, HEAD_DIM) for p in (lambda_q1, lambda_k1, lambda_q2, lambda_k2)]

    mod = _ada(jnp.concatenate([c_prompt, c_sample], axis=0), w_ada[layer], b_ada[layer])
    mod_p = [m.reshape(bp, 1, d) for m in jnp.split(mod[:bp], 6, axis=-1)]
    mod_s = [jnp.repeat(m, ts, axis=0).reshape(1, n_s, d) for m in jnp.split(mod[bp:], 6, axis=-1)]

    cos_p, sin_p = _rope_tables(jnp.arange(tp, dtype=jnp.int32))
    pos_s = past + jnp.arange(ts, dtype=jnp.int32)
    cos_s, sin_s = _rope_tables(jnp.tile(pos_s, bs))
    qk_scale = HEAD_DIM ** -0.5

    xp = x_prompt.reshape(n_p, d)
    sh1, sc1, g1, sh2, sc2, g2 = mod_p
    tm = 1024
    h = _norm_mod(xp, norm1_g[layer], sc1, sh1, 256, tp)
    (q_hm,) = _proj(h, w_in_bf, 0, D_ATTN, tm, 512, rope=(cos_p, sin_p), scale=qk_scale, hm_seq=tp)
    kt_p, k_hm = _proj(h, w_in_bf, D_ATTN, D_ATTN, tm, 512, rope=(cos_p, sin_p), t_seq=tp, hm_seq=tp)
    v_p, v_hm = _proj(h, w_in_bf, 2 * D_ATTN, D_ATTN, tm, 512, want_f32=True, hm_seq=tp)
    u_p = _glu(h, w_in_bf, 3 * D_ATTN, 3 * D_ATTN + D_CONV, D_CONV, tm, 512)
    o_p = _prompt_attn(q_hm, k_hm, v_hm, lam_params, subln_g[layer], lam_init, 512)
    u_p3 = u_p.reshape(bp, tp, D_CONV)
    yc_p = _conv_prompt(u_p3, conv_w[layer], conv_b[layer], conv_ln_g[layer], conv_ln_b[layer], 128)
    x1_p = _out_proj(o_p.reshape(n_p, D_ATTN), yc_p.reshape(n_p, D_CONV), w_out_bf, xp, g1, tm, 512, tp)
    h2_p = _norm_mod(x1_p, norm2_g[layer], sc2, sh2, 256, tp)
    y_prompt = _peer_block(h2_p, x1_p, g2, pw, tm, tp, 256, 512, 512).reshape(bp, tp, d)

    xs = x_sample.reshape(n_s, d)
    sh1, sc1, g1, sh2, sc2, g2 = mod_s
    tm = n_s
    h = _norm_mod(xs, norm1_g[layer], sc1, sh1, tm, n_s)
    (q_s,) = _proj(h, w_in_bf, 0, D_ATTN, tm, 1024, rope=(cos_s, sin_s), scale=qk_scale, want_bf=True)
    (k_s,) = _proj(h, w_in_bf, D_ATTN, D_ATTN, tm, 1024, rope=(cos_s, sin_s), want_f32=True)
    (v_s,) = _proj(h, w_in_bf, 2 * D_ATTN, D_ATTN, tm, 1024, want_f32=True)
    u_s = _glu(h, w_in_bf, 3 * D_ATTN, 3 * D_ATTN + D_CONV, D_CONV, tm, 1024)
    qblk = _block_diag_queries(q_s, bs, ts)
    knew = jnp.pad(k_s.reshape(bs, ts, D_ATTN).transpose(0, 2, 1), ((0, 0), (0, 0), (0, PAGE_SIZE - ts)))
    vnew = jnp.pad(v_s.reshape(bs, ts * N_HEADS, HEAD_W), ((0, 0), (0, (PAGE_SIZE - ts) * N_HEADS), (0, 0)))
    n_pool = cache_k.shape[1]
    cache_kt = cache_k[layer].transpose(0, 2, 3, 4, 1).reshape(n_pool, D_ATTN, PAGE_SIZE)
    cache_v2 = cache_v[layer].reshape(n_pool, PAGE_SIZE * N_HEADS, HEAD_W)
    o_s = _sample_attn(page_table, qblk, knew, vnew, cache_kt, cache_v2,
                       lam_params, subln_g[layer], lam_init, ts, 8)
    ext_s = jnp.concatenate([state_conv[layer], u_s.reshape(bs, ts, D_CONV)], axis=1)
    yc_s = _conv_sample(ext_s, conv_w[layer], conv_b[layer], conv_ln_g[layer], conv_ln_b[layer], ts)
    x1_s = _out_proj(o_s.reshape(n_s, D_ATTN).astype(BF16), yc_s.reshape(n_s, D_CONV).astype(BF16),
                     w_out_bf, xs, g1, tm, 1024, n_s)
    h2_s = _norm_mod(x1_s, norm2_g[layer], sc2, sh2, tm, n_s)
    y_sample = _peer_block(h2_s, x1_s, g2, pw, tm, n_s, n_s, n_s, 512).reshape(bs, ts, d)

    new_k_p = kt_p.reshape(1, bp, N_HEADS, 2, HEAD_DIM, tp).transpose(0, 1, 5, 2, 3, 4)
    new_v_p = v_p.reshape(1, bp, tp, N_HEADS, HEAD_W)
    new_conv_p = u_p3[:, tp - (CONV_WIDTH - 1):][None]
    new_k_s = k_s.reshape(1, bs, ts, N_HEADS, 2, HEAD_DIM)
    new_v_s = v_s.reshape(1, bs, ts, N_HEADS, HEAD_W)
    new_conv_s = ext_s[:, ts:][None]
    return (y_prompt, y_sample, new_k_p, new_v_p, new_conv_p, new_k_s, new_v_s, new_conv_s)
```
